```python
import math
import jax, jax.numpy as jnp
from jax import lax
import numpy as np

D_MODEL = 1024
BATCH = 1
SEQ = 16384
DEPTH = 2
DEC_BATCH = 32
DEC_SEQ = 1
PAST_LEN = 16384
PAGE_SIZE = 128

N_MIXERS = 2
N_POOL_LAYERS = (DEPTH + 1) // 2
N_ATTN_LAYERS = DEPTH // 2
D_FF = 2816
RMS_EPS = 1e-6
POOL_WINDOWS = (2, 4, 8, 16)
N_POOL_GROUPS = 4
POOL_GROUP_DIM = D_MODEL // N_POOL_GROUPS
POOL_STATE_LEN = max(POOL_WINDOWS) - 1
ATTN_WINDOWS = (128, 512, 2048)
ATTN_DILATIONS = (1, 4, 16)
N_ATTN_GROUPS = 3
HEAD_DIM = 64
HEADS_PER_GROUP = D_MODEL // 128
GROUP_WIDTH = HEADS_PER_GROUP * HEAD_DIM
QKV_WIDTH = N_ATTN_GROUPS * 3 * GROUP_WIDTH
N_ATTN_HEADS = N_ATTN_GROUPS * HEADS_PER_GROUP
Q_BLOCK = 128
ATTN_SCALE = HEAD_DIM ** -0.5
N_BUCKETS = 32
MAX_EXACT = N_BUCKETS // 2
MAX_DISTANCE = 2048
NEG_INF = -1e30

kernel_name = 'pool_dilated_attn_macaron_step'


def _rmsnorm(x, g):
    xf = x.astype(jnp.float32)
    y = xf * lax.rsqrt(jnp.mean(xf * xf, axis=-1, keepdims=True) + RMS_EPS)
    return (y * g.astype(jnp.float32)).astype(x.dtype)


def _ffn_half(x, g, w_gate, w_up, w_down):
    h = _rmsnorm(x, g)
    return x + 0.5 * ((jax.nn.silu(h @ w_gate) * (h @ w_up)) @ w_down)


def _t5_bucket(dist):
    distf = jnp.maximum(dist, 1).astype(jnp.float32)
    log_b = MAX_EXACT + (jnp.log(distf / MAX_EXACT) / math.log(MAX_DISTANCE / MAX_EXACT)
                         * (N_BUCKETS - MAX_EXACT)).astype(jnp.int32)
    log_b = jnp.minimum(log_b, N_BUCKETS - 1)
    return jnp.where(dist < MAX_EXACT, dist, log_b)


def _group_bias(rel_bias, g):
    n_keys = ATTN_WINDOWS[g] // ATTN_DILATIONS[g] + 1
    dist = jnp.arange(n_keys, dtype=jnp.int32) * ATTN_DILATIONS[g]
    tab = rel_bias[:, g * HEADS_PER_GROUP:(g + 1) * HEADS_PER_GROUP]
    return jnp.take(tab, _t5_bucket(dist), axis=0).T.astype(jnp.float32)


def _softmax_lse(s, valid):
    s = jnp.where(valid, s, NEG_INF)
    m = jnp.max(s, axis=-1, keepdims=True)
    e = jnp.exp(s - m)
    den = jnp.sum(e, axis=-1, keepdims=True)
    return e / den, (m + jnp.log(den))[..., 0]


def _pool_mixer(h, prefix, pos0, w_in, w_group, scale, w_out):
    B, T, D = h.shape
    u = h @ w_in
    cat = jnp.concatenate([prefix.astype(u.dtype), u], axis=1)
    c = jnp.pad(jnp.cumsum(cat.astype(jnp.float32), axis=1), ((0, 0), (1, 0), (0, 0)))
    end = c[:, POOL_STATE_LEN + 1:]
    pos = pos0 + jnp.arange(T)
    means = []
    for g, w in enumerate(POOL_WINDOWS):
        sl = slice(g * POOL_GROUP_DIM, (g + 1) * POOL_GROUP_DIM)
        start = c[:, POOL_STATE_LEN + 1 - w:POOL_STATE_LEN + 1 - w + T, sl]
        count = jnp.minimum(w, pos + 1).astype(jnp.float32)[None, :, None]
        means.append((end[..., sl] - start) / count)
    z = (jnp.concatenate(means, axis=-1) - u.astype(jnp.float32)).astype(u.dtype)
    z = jnp.einsum('btgc,gcd->btgd', z.reshape(B, T, N_POOL_GROUPS, POOL_GROUP_DIM), w_group)
    y = (z.reshape(B, T, D) * scale) @ w_out
    return y, cat[:, -POOL_STATE_LEN:]


def _dilated_attn_prompt(q, k, v, bias, window, dil):
    B, S, H, E = q.shape
    n_win = window // dil
    span = dil * Q_BLOCK
    sp = -(-S // span) * span
    nb = sp // span

    def to_blocks(a):
        a = jnp.pad(a, ((0, 0), (0, sp - S), (0, 0), (0, 0)))
        return a.reshape(B, nb, Q_BLOCK, dil, H, E).transpose(0, 3, 1, 2, 4, 5)

    def with_prev(a):
        prev = jnp.pad(a, ((0, 0), (0, 0), (1, 0), (0, 0), (0, 0), (0, 0)))[:, :, :-1]
        return jnp.concatenate([prev, a], axis=3)

    qb = to_blocks(q)
    kk = with_prev(to_blocks(k))
    vv = with_prev(to_blocks(v))
    a_idx = jnp.arange(Q_BLOCK)[:, None]
    b_idx = jnp.arange(2 * Q_BLOCK)[None, :]
    j = Q_BLOCK + a_idx - b_idx
    band = (j >= 0) & (j <= n_win)
    n_idx = jnp.arange(nb)[:, None, None]
    valid = band[None] & ((n_idx > 0) | (b_idx[None] >= Q_BLOCK))
    bias_qk = bias[:, jnp.clip(j, 0, n_win)]
    s = jnp.einsum('brnqhe,brnkhe->brnhqk', qb, kk).astype(jnp.float32) * ATTN_SCALE + bias_qk
    p, lse = _softmax_lse(s, valid[None, None, :, None])
    o = jnp.einsum('brnhqk,brnkhe->brnqhe', p, vv.astype(jnp.float32))
    o = o.transpose(0, 2, 3, 1, 4, 5).reshape(B, sp, H, E)[:, :S]
    lse = lse.transpose(0, 2, 4, 1, 3).reshape(B, sp, H)[:, :S]
    return o, lse


def _dilated_attn_step(q, k_new, v_new, k_buf, v_buf, bias, window, dil):
    T = q.shape[1]
    L = k_buf.shape[1]
    n_win = window // dil
    k_all = jnp.concatenate([k_buf.astype(k_new.dtype), k_new], axis=1)
    v_all = jnp.concatenate([v_buf.astype(v_new.dtype), v_new], axis=1)
    idx = L + jnp.arange(T)[:, None] - dil * jnp.arange(n_win + 1)[None, :]
    valid = idx >= 0
    kg = jnp.take(k_all, jnp.maximum(idx, 0), axis=1)
    vg = jnp.take(v_all, jnp.maximum(idx, 0), axis=1)
    s = jnp.einsum('bthe,btjhe->bhtj', q, kg).astype(jnp.float32) * ATTN_SCALE + bias[None, :, None, :]
    p, lse = _softmax_lse(s, valid[None, None])
    o = jnp.einsum('bhtj,btjhe->bthe', p, vg.astype(jnp.float32))
    return o, lse.transpose(0, 2, 1), k_all[:, T:], v_all[:, T:]


def _merge_groups(outs, lses, w_out, dtype):
    alpha = jax.nn.softmax(jnp.stack(lses, axis=0), axis=0)
    o = jnp.sum(alpha[..., None] * jnp.stack(outs, axis=0), axis=0)
    B, T = o.shape[:2]
    return o.reshape(B, T, GROUP_WIDTH).astype(dtype) @ w_out


def _qkv(h, w_qkv):
    B, T, _ = h.shape
    return (h @ w_qkv).reshape(B, T, N_ATTN_GROUPS, 3, HEADS_PER_GROUP, HEAD_DIM)


def setup_inputs(seed: int = 0) -> dict:
    key = jax.random.key(seed)
    ks = jax.random.split(key, 40)
    f32 = jnp.float32

    def nrm(k, shape, scale):
        return jax.random.normal(k, shape, f32) * scale

    inp = {}
    inp['x_prompt'] = nrm(ks[0], (BATCH, SEQ, D_MODEL), 1.0)
    inp['x_sample'] = nrm(ks[1], (DEC_BATCH, DEC_SEQ, D_MODEL), 1.0)
    inp['state_pool'] = nrm(ks[2], (N_POOL_LAYERS, DEC_BATCH, POOL_STATE_LEN, D_MODEL), 1.0)
    for g, w in enumerate(ATTN_WINDOWS):
        L = min(w, PAST_LEN)
        shp = (N_ATTN_LAYERS, DEC_BATCH, L, HEADS_PER_GROUP, HEAD_DIM)
        inp['cache_k_w%d' % w] = nrm(ks[3 + 2 * g], shp, 1.0)
        inp['cache_v_w%d' % w] = nrm(ks[4 + 2 * g], shp, 1.0)
    inp['ffn1_norm'] = 1.0 + nrm(ks[10], (DEPTH, D_MODEL), 0.05)
    inp['ffn1_w_gate'] = nrm(ks[11], (DEPTH, D_MODEL, D_FF), D_MODEL ** -0.5)
    inp['ffn1_w_up'] = nrm(ks[12], (DEPTH, D_MODEL, D_FF), D_MODEL ** -0.5)
    inp['ffn1_w_down'] = nrm(ks[13], (DEPTH, D_FF, D_MODEL), D_FF ** -0.5)
    inp['mix_norm'] = 1.0 + nrm(ks[14], (DEPTH, D_MODEL), 0.05)
    inp['pool_w_in'] = nrm(ks[15], (N_POOL_LAYERS, D_MODEL, D_MODEL), D_MODEL ** -0.5)
    inp['pool_w_group'] = nrm(ks[16], (N_POOL_LAYERS, N_POOL_GROUPS, POOL_GROUP_DIM, POOL_GROUP_DIM), POOL_GROUP_DIM ** -0.5)
    inp['pool_scale'] = 1.0 + nrm(ks[17], (N_POOL_LAYERS, D_MODEL), 0.05)
    inp['pool_w_out'] = nrm(ks[18], (N_POOL_LAYERS, D_MODEL, D_MODEL), D_MODEL ** -0.5)
    inp['attn_w_qkv'] = nrm(ks[19], (N_ATTN_LAYERS, D_MODEL, QKV_WIDTH), D_MODEL ** -0.5)
    inp['attn_w_out'] = nrm(ks[20], (N_ATTN_LAYERS, GROUP_WIDTH, D_MODEL), GROUP_WIDTH ** -0.5)
    inp['rel_bias'] = nrm(ks[21], (N_BUCKETS, N_ATTN_HEADS), 0.3)
    inp['ffn2_norm'] = 1.0 + nrm(ks[22], (DEPTH, D_MODEL), 0.05)
    inp['ffn2_w_gate'] = nrm(ks[23], (DEPTH, D_MODEL, D_FF), D_MODEL ** -0.5)
    inp['ffn2_w_up'] = nrm(ks[24], (DEPTH, D_MODEL, D_FF), D_MODEL ** -0.5)
    inp['ffn2_w_down'] = nrm(ks[25], (DEPTH, D_FF, D_MODEL), D_FF ** -0.5)
    inp['final_norm'] = 1.0 + nrm(ks[26], (D_MODEL,), 0.05)
    return inp


def reference(x_prompt, x_sample, state_pool, cache_k_w128, cache_v_w128, cache_k_w512, cache_v_w512,
              cache_k_w2048, cache_v_w2048, ffn1_norm, ffn1_w_gate, ffn1_w_up, ffn1_w_down, mix_norm,
              pool_w_in, pool_w_group, pool_scale, pool_w_out, attn_w_qkv, attn_w_out, rel_bias,
              ffn2_norm, ffn2_w_gate, ffn2_w_up, ffn2_w_down, final_norm):
    cache_k = (cache_k_w128, cache_k_w512, cache_k_w2048)
    cache_v = (cache_v_w128, cache_v_w512, cache_v_w2048)
    biases = [_group_bias(rel_bias, g) for g in range(N_ATTN_GROUPS)]
    xp, xs = x_prompt, x_sample
    S = xp.shape[1]
    pool_p, pool_s = [], []
    kp = [[] for _ in range(N_ATTN_GROUPS)]
    vp = [[] for _ in range(N_ATTN_GROUPS)]
    ksn = [[] for _ in range(N_ATTN_GROUPS)]
    vsn = [[] for _ in range(N_ATTN_GROUPS)]
    for i in range(DEPTH):
        xp = _ffn_half(xp, ffn1_norm[i], ffn1_w_gate[i], ffn1_w_up[i], ffn1_w_down[i])
        xs = _ffn_half(xs, ffn1_norm[i], ffn1_w_gate[i], ffn1_w_up[i], ffn1_w_down[i])
        hp = _rmsnorm(xp, mix_norm[i])
        hs = _rmsnorm(xs, mix_norm[i])
        li = i // N_MIXERS
        if i % N_MIXERS == 0:
            zeros = jnp.zeros((hp.shape[0], POOL_STATE_LEN, D_MODEL), hp.dtype)
            yp, st_p = _pool_mixer(hp, zeros, 0, pool_w_in[li], pool_w_group[li], pool_scale[li], pool_w_out[li])
            ys, st_s = _pool_mixer(hs, state_pool[li], PAST_LEN, pool_w_in[li], pool_w_group[li],
                                   pool_scale[li], pool_w_out[li])
            pool_p.append(st_p)
            pool_s.append(st_s)
        else:
            qkv_p = _qkv(hp, attn_w_qkv[li])
            qkv_s = _qkv(hs, attn_w_qkv[li])
            outs_p, lses_p, outs_s, lses_s = [], [], [], []
            for g in range(N_ATTN_GROUPS):
                w, d = ATTN_WINDOWS[g], ATTN_DILATIONS[g]
                q, k, v = qkv_p[:, :, g, 0], qkv_p[:, :, g, 1], qkv_p[:, :, g, 2]
                o, lse = _dilated_attn_prompt(q, k, v, biases[g], w, d)
                outs_p.append(o)
                lses_p.append(lse)
                Lp = min(w, S)
                kp[g].append(k[:, S - Lp:])
                vp[g].append(v[:, S - Lp:])
                o, lse, kb, vb = _dilated_attn_step(qkv_s[:, :, g, 0], qkv_s[:, :, g, 1], qkv_s[:, :, g, 2],
                                                    cache_k[g][li], cache_v[g][li], biases[g], w, d)
                outs_s.append(o)
                lses_s.append(lse)
                ksn[g].append(kb)
                vsn[g].append(vb)
            yp = _merge_groups(outs_p, lses_p, attn_w_out[li], hp.dtype)
            ys = _merge_groups(outs_s, lses_s, attn_w_out[li], hs.dtype)
        xp = xp + yp
        xs = xs + ys
        xp = _ffn_half(xp, ffn2_norm[i], ffn2_w_gate[i], ffn2_w_up[i], ffn2_w_down[i])
        xs = _ffn_half(xs, ffn2_norm[i], ffn2_w_gate[i], ffn2_w_up[i], ffn2_w_down[i])
    y_prompt = _rmsnorm(xp, final_norm)
    y_sample = _rmsnorm(xs, final_norm)
    return (y_prompt, y_sample, jnp.stack(pool_p), jnp.stack(pool_s),
            jnp.stack(kp[0]), jnp.stack(vp[0]), jnp.stack(ksn[0]), jnp.stack(vsn[0]),
            jnp.stack(kp[1]), jnp.stack(vp[1]), jnp.stack(ksn[1]), jnp.stack(vsn[1]),
            jnp.stack(kp[2]), jnp.stack(vp[2]), jnp.stack(ksn[2]), jnp.stack(vsn[2]))
```

```python
import functools
import math

import numpy as np
import jax
import jax.numpy as jnp
from jax import lax
from jax.experimental import pallas as pl
from jax.experimental.pallas import tpu as pltpu

F32 = jnp.float32
BF16 = jnp.bfloat16

D_MODEL = 1024
D_FF = 2816
RMS_EPS = 1e-6
POOL_WINDOWS = (2, 4, 8, 16)
POOL_GROUP_DIM = D_MODEL // len(POOL_WINDOWS)
POOL_STATE_LEN = max(POOL_WINDOWS) - 1
ATTN_WINDOWS = (128, 512, 2048)
ATTN_DILATIONS = (1, 4, 16)
N_GROUPS = 3
HEAD_DIM = 64
HEADS = 8
GROUP_WIDTH = HEADS * HEAD_DIM
QKV_WIDTH = N_GROUPS * 3 * GROUP_WIDTH
Q_BLOCK = 128
N_KEYS = Q_BLOCK + 1
ATTN_SCALE = HEAD_DIM ** -0.5
N_BUCKETS = 32
MAX_EXACT = N_BUCKETS // 2
MAX_DISTANCE = 2048
NEG_INF = -1e30

V7X_LANES = 128
V7X_VMEM_BYTES = 64 * 1024 * 1024
V7X_VMEM_RESERVE = 6 * 1024 * 1024
ROW_TILE = 512


def _vmem_limit(estimate_bytes):
    return int(min(V7X_VMEM_BYTES - V7X_VMEM_RESERVE, estimate_bytes))


def _params(n_axes, vmem_bytes):
    return pltpu.CompilerParams(
        dimension_semantics=("arbitrary",) * n_axes,
        vmem_limit_bytes=_vmem_limit(vmem_bytes))


def _resident(shape):
    return pl.BlockSpec(shape, lambda *_: (0,) * len(shape), pipeline_mode=pl.Buffered(1))


def _rms(x, g):
    return x * lax.rsqrt(jnp.mean(x * x, axis=-1, keepdims=True) + RMS_EPS) * g


def _ffn_kernel(x_ref, g_ref, wg_ref, wu_ref, wd_ref, *refs, post):
    pg_ref, out_refs = (None, refs) if post == "none" else (refs[0], refs[1:])
    x = x_ref[...]
    h = _rms(x, g_ref[...]).astype(BF16)
    gate = jnp.dot(h, wg_ref[...], preferred_element_type=F32)
    up = jnp.dot(h, wu_ref[...], preferred_element_type=F32)
    a = (gate * jax.nn.sigmoid(gate) * up).astype(BF16)
    y = x + 0.5 * jnp.dot(a, wd_ref[...], preferred_element_type=F32)
    if post == "none":
        out_refs[0][...] = y
    elif post == "mixer_input":
        out_refs[0][...] = y
        out_refs[1][...] = _rms(y, pg_ref[...]).astype(BF16)
    else:
        out_refs[0][...] = _rms(y, pg_ref[...])


def _ffn(x, g, wg, wu, wd, post_g, post, name):
    rows = x.shape[0]
    tm = min(ROW_TILE, rows)
    row_spec = pl.BlockSpec((tm, D_MODEL), lambda i: (i, 0))
    vec_spec = _resident((1, D_MODEL))
    out_shape = [jax.ShapeDtypeStruct((rows, D_MODEL), F32)]
    out_specs = [row_spec]
    if post == "mixer_input":
        out_shape.append(jax.ShapeDtypeStruct((rows, D_MODEL), BF16))
        out_specs.append(row_spec)
    weights = 3 * D_MODEL * D_FF * 2
    tiles = tm * D_MODEL * (4 * 2 + 4 * 2 + 2 * 2)
    temps = tm * D_FF * (4 + 4 + 4 + 2) + tm * D_MODEL * 12
    post_in = () if post == "none" else (post_g,)
    return pl.pallas_call(
        functools.partial(_ffn_kernel, post=post),
        out_shape=out_shape,
        grid=(rows // tm,),
        in_specs=[row_spec, vec_spec, _resident((D_MODEL, D_FF)), _resident((D_MODEL, D_FF)),
                  _resident((D_FF, D_MODEL))] + [vec_spec] * len(post_in),
        out_specs=out_specs,
        compiler_params=_params(1, weights + tiles + temps),
        name=name,
    )(x, g, wg, wu, wd, *post_in)


def _pool_tail(x, u, sums_minus_u, counts, wgrp_ref, scale_ref, wout_ref):
    y = x
    for g, w in enumerate(POOL_WINDOWS):
        cols = slice(g * POOL_GROUP_DIM, (g + 1) * POOL_GROUP_DIM)
        ug = u[:, cols]
        z = ((sums_minus_u[g] + ug) / counts[g] - ug).astype(BF16)
        z = jnp.dot(z, wgrp_ref[g], preferred_element_type=F32) * scale_ref[:, cols]
        y = y + jnp.dot(z.astype(BF16), wout_ref[cols, :], preferred_element_type=F32)
    return y


def _pool_prompt_kernel(x_ref, h_ref, win_ref, wgrp_ref, scale_ref, wout_ref,
                        xo_ref, tail_ref, ubuf_ref, *, tm):
    i = pl.program_id(0)
    hist = POOL_STATE_LEN + 1

    @pl.when(i == 0)
    def _():
        ubuf_ref[0:hist, :] = jnp.zeros((hist, D_MODEL), F32)

    u = jnp.dot(h_ref[...], win_ref[...], preferred_element_type=F32)
    ubuf_ref[hist:hist + tm, :] = u
    pos = i * tm + lax.broadcasted_iota(jnp.int32, (tm, 1), 0)
    sums, counts = [], []
    for g, w in enumerate(POOL_WINDOWS):
        cols = slice(g * POOL_GROUP_DIM, (g + 1) * POOL_GROUP_DIM)
        acc = ubuf_ref[hist - 1:hist - 1 + tm, cols]
        for j in range(2, w):
            acc = acc + ubuf_ref[hist - j:hist - j + tm, cols]
        sums.append(acc)
        counts.append(jnp.minimum(w, pos + 1).astype(F32))
    xo_ref[...] = _pool_tail(x_ref[...], u, sums, counts, wgrp_ref, scale_ref, wout_ref)
    last = ubuf_ref[tm:tm + hist, :]
    tail_ref[...] = last
    ubuf_ref[0:hist, :] = last


def _pool_prompt(x, h, win, wgrp, scale, wout):
    rows = x.shape[0]
    tm = ROW_TILE
    hist = POOL_STATE_LEN + 1
    row_spec = pl.BlockSpec((tm, D_MODEL), lambda i: (i, 0))
    weights = (2 * D_MODEL * D_MODEL + 4 * POOL_GROUP_DIM * POOL_GROUP_DIM) * 2
    tiles = tm * D_MODEL * (4 * 2 + 2 * 2 + 4 * 2) + (tm + hist) * D_MODEL * 4
    temps = tm * D_MODEL * 4 * 6
    return pl.pallas_call(
        functools.partial(_pool_prompt_kernel, tm=tm),
        out_shape=[jax.ShapeDtypeStruct((rows, D_MODEL), F32),
                   jax.ShapeDtypeStruct((hist, D_MODEL), F32)],
        grid=(rows // tm,),
        in_specs=[row_spec, row_spec, _resident((D_MODEL, D_MODEL)),
                  _resident((4, POOL_GROUP_DIM, POOL_GROUP_DIM)), _resident((1, D_MODEL)),
                  _resident((D_MODEL, D_MODEL))],
        out_specs=[row_spec, pl.BlockSpec((hist, D_MODEL), lambda i: (0, 0))],
        scratch_shapes=[pltpu.VMEM((tm + hist, D_MODEL), F32)],
        compiler_params=_params(1, weights + tiles + temps),
        name="pool_prompt",
    )(x, h, win, wgrp, scale, wout)


def _pool_step_kernel(x_ref, h_ref, st_ref, win_ref, wgrp_ref, scale_ref, wout_ref,
                      xo_ref, sto_ref, *, past_len):
    u = jnp.dot(h_ref[...], win_ref[...], preferred_element_type=F32)
    sums, counts = [], []
    for g, w in enumerate(POOL_WINDOWS):
        cols = slice(g * POOL_GROUP_DIM, (g + 1) * POOL_GROUP_DIM)
        acc = st_ref[POOL_STATE_LEN - 1, :, cols]
        for j in range(2, w):
            acc = acc + st_ref[POOL_STATE_LEN - j, :, cols]
        sums.append(acc)
        counts.append(float(min(w, past_len + 1)))
    xo_ref[...] = _pool_tail(x_ref[...], u, sums, counts, wgrp_ref, scale_ref, wout_ref)
    for k in range(POOL_STATE_LEN - 1):
        sto_ref[k] = st_ref[k + 1]
    sto_ref[POOL_STATE_LEN - 1] = u


def _pool_step(x, h, state, win, wgrp, scale, wout, past_len):
    rows = x.shape[0]
    full = lambda shape: pl.BlockSpec(shape, lambda i: (0,) * len(shape))
    return pl.pallas_call(
        functools.partial(_pool_step_kernel, past_len=past_len),
        out_shape=[jax.ShapeDtypeStruct((rows, D_MODEL), F32),
                   jax.ShapeDtypeStruct(state.shape, F32)],
        grid=(1,),
        in_specs=[full((rows, D_MODEL)), full((rows, D_MODEL)), full(state.shape),
                  full((D_MODEL, D_MODEL)), full((4, POOL_GROUP_DIM, POOL_GROUP_DIM)),
                  full((1, D_MODEL)), full((D_MODEL, D_MODEL))],
        out_specs=[full((rows, D_MODEL)), full(state.shape)],
        compiler_params=_params(1, 32 * 1024 * 1024),
        name="pool_step",
    )(x, h, state, win, wgrp, scale, wout)


def _qkv_kernel(h_ref, w_ref, *refs, tm, dilations):
    outs, slab_ref = refs[:N_GROUPS], refs[N_GROUPS]
    h = h_ref[...]
    width = 3 * GROUP_WIDTH
    for g, (o_ref, d) in enumerate(zip(outs, dilations)):
        res = jnp.dot(h, w_ref[:, g * width:(g + 1) * width], preferred_element_type=F32)
        if d == 1:
            o_ref[0] = res
            continue
        for c in range(width // V7X_LANES):
            slab_ref[c] = res[:, c * V7X_LANES:(c + 1) * V7X_LANES]
        for r in range(d):
            for c in range(width // V7X_LANES):
                o_ref[r, :, c * V7X_LANES:(c + 1) * V7X_LANES] = slab_ref[c, pl.ds(r, tm // d, stride=d), :]


def _qkv(h, w, dilations, name):
    rows = h.shape[0]
    tm = min(ROW_TILE, rows)
    width = 3 * GROUP_WIDTH
    weights = D_MODEL * QKV_WIDTH * 2
    tiles = tm * D_MODEL * 2 * 2 + tm * QKV_WIDTH * 4 * 2
    temps = tm * width * 4 * 3
    return pl.pallas_call(
        functools.partial(_qkv_kernel, tm=tm, dilations=dilations),
        out_shape=[jax.ShapeDtypeStruct((d, rows // d, width), F32) for d in dilations],
        grid=(rows // tm,),
        in_specs=[pl.BlockSpec((tm, D_MODEL), lambda i: (i, 0)), _resident((D_MODEL, QKV_WIDTH))],
        out_specs=[pl.BlockSpec((d, tm // d, width), lambda i: (0, i, 0)) for d in dilations],
        scratch_shapes=[pltpu.VMEM((width // V7X_LANES, tm, V7X_LANES), F32)],
        compiler_params=_params(1, weights + tiles + temps),
        name=name,
    )(h, w)


def _kv_tail_kernel(h_ref, wk_ref, wv_ref, o_ref):
    h = h_ref[...]
    o_ref[0] = jnp.dot(h, wk_ref[...], preferred_element_type=F32).T
    o_ref[1] = jnp.dot(h, wv_ref[...], preferred_element_type=F32).T


def _kv_tail(h, w, g, window):
    rows = h.shape[0]
    tm = min(window, 256)
    first = (rows - window) // tm
    col = 3 * g
    return pl.pallas_call(
        _kv_tail_kernel,
        out_shape=jax.ShapeDtypeStruct((2, GROUP_WIDTH, window), F32),
        grid=(window // tm,),
        in_specs=[pl.BlockSpec((tm, D_MODEL), lambda i: (first + i, 0)),
                  pl.BlockSpec((D_MODEL, GROUP_WIDTH), lambda i: (0, col + 1)),
                  pl.BlockSpec((D_MODEL, GROUP_WIDTH), lambda i: (0, col + 2))],
        out_specs=pl.BlockSpec((2, GROUP_WIDTH, tm), lambda i: (0, 0, i)),
        compiler_params=_params(1, 16 * 1024 * 1024),
        name="kv_tail_g%d" % g,
    )(h, w, w)


def _bucket_table():
    out = np.zeros((N_GROUPS, N_KEYS), np.int32)
    for g, d in enumerate(ATTN_DILATIONS):
        dist = np.arange(N_KEYS, dtype=np.int32) * d
        distf = np.maximum(dist, 1).astype(np.float32)
        log_b = MAX_EXACT + (np.log(distf / np.float32(MAX_EXACT)) / np.float32(math.log(MAX_DISTANCE / MAX_EXACT))
                             * np.float32(N_BUCKETS - MAX_EXACT)).astype(np.int32)
        log_b = np.minimum(log_b, N_BUCKETS - 1)
        out[g] = np.where(dist < MAX_EXACT, dist, log_b)
    return out


def _band_offsets():
    a = np.arange(Q_BLOCK)[:, None]
    b = np.arange(2 * Q_BLOCK)[None, :]
    j = Q_BLOCK + a - b
    return np.where((j >= 0) & (j <= Q_BLOCK), j, -1).astype(np.int32)


def _bias_kernel(tab_ref, bidx_ref, l0_ref, l1_ref, l2_ref, band_ref, s0_ref, s1_ref, s2_ref, new_ref, *, buckets):
    sub = lax.broadcasted_iota(jnp.int32, (HEADS, V7X_LANES), 0)
    col = lax.broadcasted_iota(jnp.int32, (Q_BLOCK, 2 * Q_BLOCK), 1)
    for g, (lidx_ref, step_ref) in enumerate(((l0_ref, s0_ref), (l1_ref, s1_ref), (l2_ref, s2_ref))):
        bidx = bidx_ref[g]
        lidx = lidx_ref[...]
        used = sorted(set(int(v) for v in buckets[g]))
        t = jnp.zeros((HEADS, V7X_LANES), F32)
        for h in range(HEADS):
            t = jnp.where(sub == h, tab_ref[int(buckets[g][0]), g * HEADS + h], t)
        new_ref[g] = t
        for h in range(HEADS):
            tile = jnp.full(bidx.shape, NEG_INF, F32)
            row = jnp.full(lidx.shape, NEG_INF, F32)
            for v in used:
                tile = jnp.where(bidx == v, tab_ref[v, g * HEADS + h], tile)
                row = jnp.where(lidx == v, tab_ref[v, g * HEADS + h], row)
            band_ref[g, 1, h] = tile
            band_ref[g, 0, h] = jnp.where(col >= Q_BLOCK, tile, NEG_INF)
            step_ref[h:h + 1, :] = row


def _bias_tables(rel_bias):
    buckets = _bucket_table()
    band = _band_offsets()
    bidx = np.stack([np.where(band >= 0, buckets[g][np.maximum(band, 0)], -1) for g in range(N_GROUPS)])
    lidx = []
    for g, (w, d) in enumerate(zip(ATTN_WINDOWS, ATTN_DILATIONS)):
        pos = np.arange(w)
        lidx.append(np.where(pos % d == 0, buckets[g][(w - pos) // d], -1).astype(np.int32)[None])
    vmem = pl.BlockSpec(memory_space=pltpu.VMEM)
    return pl.pallas_call(
        functools.partial(_bias_kernel, buckets=buckets),
        out_shape=[jax.ShapeDtypeStruct((N_GROUPS, 2, HEADS, Q_BLOCK, 2 * Q_BLOCK), F32)]
        + [jax.ShapeDtypeStruct((HEADS, w), F32) for w in ATTN_WINDOWS]
        + [jax.ShapeDtypeStruct((N_GROUPS, HEADS, V7X_LANES), F32)],
        in_specs=[pl.BlockSpec(memory_space=pltpu.SMEM)] + [vmem] * 4,
        out_specs=[vmem] * 5,
        compiler_params=pltpu.CompilerParams(vmem_limit_bytes=_vmem_limit(24 * 1024 * 1024)),
        name="bias_tables",
    )(rel_bias, jnp.asarray(bidx), *[jnp.asarray(l) for l in lidx])


def _attn_prompt_kernel(q_ref, kp_ref, kc_ref, vp_ref, vc_ref, bias_ref, o_ref, lse_ref):
    low = lax.broadcasted_iota(jnp.int32, (1, V7X_LANES), 1) < HEAD_DIM
    q = q_ref[...] * ATTN_SCALE
    k = jnp.concatenate([kp_ref[...], kc_ref[...]], axis=0).astype(BF16)
    v = jnp.concatenate([vp_ref[...], vc_ref[...]], axis=0)
    for p in range(HEADS // 2):
        cols = slice(p * V7X_LANES, (p + 1) * V7X_LANES)
        qp, kp, vp = q[:, cols], k[:, cols], v[:, cols]
        o_pair = jnp.zeros((Q_BLOCK, V7X_LANES), F32)
        lses = []
        for half in range(2):
            mine = low if half == 0 else jnp.logical_not(low)
            qh = jnp.where(mine, qp, 0.0).astype(BF16)
            vh = jnp.where(mine, vp, 0.0).astype(BF16)
            s = lax.dot_general(qh, kp, (((1,), (1,)), ((), ())), preferred_element_type=F32)
            s = s + bias_ref[2 * p + half]
            m = jnp.max(s, axis=-1, keepdims=True)
            e = jnp.exp(s - m)
            den = jnp.sum(e, axis=-1, keepdims=True)
            prob = (e * (1.0 / den)).astype(BF16)
            o_pair = o_pair + jnp.dot(prob, vh, preferred_element_type=F32)
            lses.append(m + jnp.log(den))
        o_ref[:, cols] = o_pair
        lse_ref[:, cols] = jnp.where(low, lses[0], lses[1])


def _attn_prompt(qkv, band_bias, g):
    d, rows, _ = qkv.shape
    nblk = rows // Q_BLOCK

    def spec(which, prev):
        if prev:
            return pl.BlockSpec((None, Q_BLOCK, GROUP_WIDTH), lambda r, n: (r, jnp.maximum(n - 1, 0), which))
        return pl.BlockSpec((None, Q_BLOCK, GROUP_WIDTH), lambda r, n: (r, n, which))

    out_spec = pl.BlockSpec((None, Q_BLOCK, GROUP_WIDTH), lambda r, n: (r, n, 0))
    bias_spec = pl.BlockSpec((None, None, HEADS, Q_BLOCK, 2 * Q_BLOCK),
                             lambda r, n: (g, jnp.minimum(n, 1), 0, 0, 0))
    return pl.pallas_call(
        _attn_prompt_kernel,
        out_shape=[jax.ShapeDtypeStruct((d, rows, GROUP_WIDTH), F32)] * 2,
        grid=(d, nblk),
        in_specs=[spec(0, False), spec(1, True), spec(1, False), spec(2, True), spec(2, False), bias_spec],
        out_specs=[out_spec, out_spec],
        compiler_params=_params(2, 24 * 1024 * 1024),
        name="attn_prompt_g%d" % g,
    )(qkv, qkv, qkv, qkv, qkv, band_bias)


def _attn_step_kernel(q_ref, kn_ref, vn_ref, kcol_ref, vcol_ref, kc_ref, vc_ref, bias_ref, bnew_ref,
                      o_ref, lse_ref, ko_ref, vo_ref, *, bb, length):
    sub = lax.broadcasted_iota(jnp.int32, (HEADS, 1), 0)
    is_last = lax.broadcasted_iota(jnp.int32, (1, length), 1) == length - 1
    for b in range(bb):
        q = q_ref[b] * ATTN_SCALE
        qb = q.astype(BF16)
        s = jnp.zeros((HEADS, length), F32)
        for h in range(HEADS):
            kh = kc_ref[b, h]
            s = jnp.where(sub == h, jnp.dot(qb, kh.astype(BF16), preferred_element_type=F32), s)
            ko_ref[b, h] = jnp.where(is_last, kcol_ref[b, h], pltpu.roll(kh, length - 1, 1))
        s = s + bias_ref[...]
        s_new = jnp.sum(q * kn_ref[b], axis=-1, keepdims=True) + bnew_ref[:, 0:1]
        m = jnp.maximum(jnp.max(s, axis=-1, keepdims=True), s_new)
        e = jnp.exp(s - m)
        e_new = jnp.exp(s_new - m)
        den = jnp.sum(e, axis=-1, keepdims=True) + e_new
        inv = 1.0 / den
        prob = (e * inv).astype(BF16)
        o = jnp.zeros((HEADS, HEAD_DIM), F32)
        for h in range(HEADS):
            vh = vc_ref[b, h]
            o_all = lax.dot_general(prob, vh.astype(BF16), (((1,), (1,)), ((), ())),
                                    preferred_element_type=F32)
            o = jnp.where(sub == h, o_all, o)
            vo_ref[b, h] = jnp.where(is_last, vcol_ref[b, h], pltpu.roll(vh, length - 1, 1))
        o_ref[b] = o + (e_new * inv) * vn_ref[b]
        lse_ref[b] = jnp.broadcast_to(m + jnp.log(den), (HEADS, HEAD_DIM))


def _attn_step(q, k_new, v_new, cache_k, cache_v, bias_row, bias_new, g, layer):
    batch = q.shape[0]
    length = cache_k.shape[-1]
    bb = max(1, min(batch, (4 * 1024 * 1024) // (HEADS * HEAD_DIM * length * 4)))
    small = pl.BlockSpec((bb, HEADS, HEAD_DIM), lambda i: (i, 0, 0))
    column = pl.BlockSpec((bb, HEADS, HEAD_DIM, 1), lambda i: (i, 0, 0, 0))
    cache_spec = pl.BlockSpec((None, bb, HEADS, HEAD_DIM, length), lambda i: (layer, i, 0, 0, 0))
    out_cache_spec = pl.BlockSpec((None, bb, HEADS, HEAD_DIM, length), lambda i: (0, i, 0, 0, 0))
    cache_block = bb * HEADS * HEAD_DIM * length * 4
    column_block = bb * HEADS * HEAD_DIM * V7X_LANES * 4
    return pl.pallas_call(
        functools.partial(_attn_step_kernel, bb=bb, length=length),
        out_shape=[jax.ShapeDtypeStruct((batch, HEADS, HEAD_DIM), F32)] * 2
        + [jax.ShapeDtypeStruct((1,) + cache_k.shape[1:], F32)] * 2,
        grid=(batch // bb,),
        in_specs=[small, small, small, column, column, cache_spec, cache_spec,
                  pl.BlockSpec((HEADS, length), lambda i: (0, 0)),
                  pl.BlockSpec((None, HEADS, V7X_LANES), lambda i: (g, 0, 0))],
        out_specs=[small, small, out_cache_spec, out_cache_spec],
        compiler_params=_params(1, 8 * cache_block + 4 * column_block + 8 * 1024 * 1024),
        name="attn_step_g%d" % g,
    )(q, k_new, v_new, k_new[..., None], v_new[..., None], cache_k, cache_v, bias_row, bias_new)


def _merge_kernel(x_ref, *refs, tm, dilations):
    w_ref, xo_ref, slab_ref = refs[2 * N_GROUPS:]
    n_slabs = GROUP_WIDTH // V7X_LANES

    def token_order(ref, d, slot):
        if d == 1:
            return ref[0]
        for r in range(d):
            for c in range(n_slabs):
                slab_ref[slot, c, pl.ds(r, tm // d, stride=d), :] = ref[r, :, c * V7X_LANES:(c + 1) * V7X_LANES]
        return jnp.concatenate([slab_ref[slot, c] for c in range(n_slabs)], axis=1)

    os_ = [token_order(refs[g], d, g) for g, d in enumerate(dilations)]
    ls = [token_order(refs[N_GROUPS + g], d, N_GROUPS + g) for g, d in enumerate(dilations)]
    m = jnp.maximum(jnp.maximum(ls[0], ls[1]), ls[2])
    es = [jnp.exp(l - m) for l in ls]
    inv = 1.0 / (es[0] + es[1] + es[2])
    o = es[0] * inv * os_[0] + es[1] * inv * os_[1] + es[2] * inv * os_[2]
    xo_ref[...] = x_ref[...] + jnp.dot(o.astype(BF16), w_ref[...], preferred_element_type=F32)


def _merge(x, outs, lses, w, dilations, name):
    rows = x.shape[0]
    tm = min(ROW_TILE, rows)
    row_spec = pl.BlockSpec((tm, D_MODEL), lambda i: (i, 0))
    grp_specs = [pl.BlockSpec((d, tm // d, GROUP_WIDTH), lambda i: (0, i, 0)) for d in dilations]
    tiles = tm * D_MODEL * 4 * 4 + tm * GROUP_WIDTH * 4 * (2 * 6 + 6) + GROUP_WIDTH * D_MODEL * 2
    temps = tm * GROUP_WIDTH * 4 * 8
    return pl.pallas_call(
        functools.partial(_merge_kernel, tm=tm, dilations=dilations),
        out_shape=jax.ShapeDtypeStruct((rows, D_MODEL), F32),
        grid=(rows // tm,),
        in_specs=[row_spec] + grp_specs * 2 + [_resident((GROUP_WIDTH, D_MODEL))],
        out_specs=row_spec,
        scratch_shapes=[pltpu.VMEM((2 * N_GROUPS, GROUP_WIDTH // V7X_LANES, tm, V7X_LANES), F32)],
        compiler_params=_params(1, tiles + temps),
        name=name,
    )(x, *outs, *lses, w)


def kernel(x_prompt, x_sample, state_pool, cache_k_w128, cache_v_w128, cache_k_w512, cache_v_w512,
           cache_k_w2048, cache_v_w2048, ffn1_norm, ffn1_w_gate, ffn1_w_up, ffn1_w_down, mix_norm,
           pool_w_in, pool_w_group, pool_scale, pool_w_out, attn_w_qkv, attn_w_out, rel_bias,
           ffn2_norm, ffn2_w_gate, ffn2_w_up, ffn2_w_down, final_norm):
    seq = x_prompt.shape[1]
    batch = x_sample.shape[0]
    past_len = cache_k_w2048.shape[2]
    cache_k = (cache_k_w128, cache_k_w512, cache_k_w2048)
    cache_v = (cache_v_w128, cache_v_w512, cache_v_w2048)
    bf = lambda w: w.astype(BF16)
    vec = lambda v: v.reshape(1, D_MODEL)

    xp = x_prompt.reshape(seq, D_MODEL)
    xs = x_sample.reshape(batch, D_MODEL)
    band_bias, *step_bias, new_bias = _bias_tables(rel_bias)
    to_pos_minor = lambda c: jnp.transpose(c, (0, 1, 3, 4, 2))
    from_pos_minor = lambda c: jnp.transpose(c, (0, 1, 4, 2, 3))

    def ffn(x, which, layer, post_g, post, name):
        norm, wg, wu, wd = ((ffn1_norm, ffn1_w_gate, ffn1_w_up, ffn1_w_down) if which == 1 else
                            (ffn2_norm, ffn2_w_gate, ffn2_w_up, ffn2_w_down))
        post_g = None if post_g is None else vec(post_g)
        return _ffn(x, vec(norm[layer]), bf(wg[layer]), bf(wu[layer]), bf(wd[layer]), post_g, post, name)

    xp, hp = ffn(xp, 1, 0, mix_norm[0], "mixer_input", "ffn1_l0_prompt")
    xs, hs = ffn(xs, 1, 0, mix_norm[0], "mixer_input", "ffn1_l0_step")
    pool_w = (bf(pool_w_in[0]), bf(pool_w_group[0]), vec(pool_scale[0]), bf(pool_w_out[0]))
    xp, tail = _pool_prompt(xp, hp, *pool_w)
    xs, pool_state_s = _pool_step(xs, hs, jnp.swapaxes(state_pool[0], 0, 1), *pool_w, past_len)
    pool_state_p = tail[1:][None, None]
    pool_state_s = jnp.swapaxes(pool_state_s, 0, 1)[None]
    xp, = ffn(xp, 2, 0, None, "none", "ffn2_l0_prompt")
    xs, = ffn(xs, 2, 0, None, "none", "ffn2_l0_step")

    xp, hp = ffn(xp, 1, 1, mix_norm[1], "mixer_input", "ffn1_l1_prompt")
    xs, hs = ffn(xs, 1, 1, mix_norm[1], "mixer_input", "ffn1_l1_step")
    wqkv = bf(attn_w_qkv[0])
    wout = bf(attn_w_out[0])
    no_dilation = (1,) * N_GROUPS
    qkv_p = _qkv(hp, wqkv, ATTN_DILATIONS, "qkv_prompt")
    qkv_s = _qkv(hs, wqkv, no_dilation, "qkv_step")

    outs_p, lses_p, outs_s, lses_s, kv_prompt, rolled = [], [], [], [], [], []
    for g, w in enumerate(ATTN_WINDOWS):
        o, lse = _attn_prompt(qkv_p[g], band_bias, g)
        outs_p.append(o)
        lses_p.append(lse)
        keep = min(w, seq)
        kv = _kv_tail(hp, wqkv, g, keep).reshape(2, 1, 1, HEADS, HEAD_DIM, keep)
        kv_prompt.append((from_pos_minor(kv[0]), from_pos_minor(kv[1])))
        q, k_new, v_new = (qkv_s[g][0, :, t * GROUP_WIDTH:(t + 1) * GROUP_WIDTH].reshape(batch, HEADS, HEAD_DIM)
                           for t in range(3))
        o, lse, k_roll, v_roll = _attn_step(q, k_new, v_new, to_pos_minor(cache_k[g]), to_pos_minor(cache_v[g]),
                                            step_bias[g], new_bias, g, 0)
        outs_s.append(o.reshape(1, batch, GROUP_WIDTH))
        lses_s.append(lse.reshape(1, batch, GROUP_WIDTH))
        rolled += [from_pos_minor(k_roll), from_pos_minor(v_roll)]
    xp = _merge(xp, outs_p, lses_p, wout, ATTN_DILATIONS, "merge_prompt")
    xs = _merge(xs, outs_s, lses_s, wout, no_dilation, "merge_step")

    yp, = ffn(xp, 2, 1, final_norm, "final_norm", "ffn2_l1_prompt")
    ys, = ffn(xs, 2, 1, final_norm, "final_norm", "ffn2_l1_step")

    return (yp.reshape(x_prompt.shape), ys.reshape(x_sample.shape), pool_state_p, pool_state_s,
            kv_prompt[0][0], kv_prompt[0][1], rolled[0], rolled[1],
            kv_prompt[1][0], kv_prompt[1][1], rolled[2], rolled[3],
            kv_prompt[2][0], kv_prompt[2][1], rolled[4], rolled[5])
```

```python
import functools
import math

import numpy as np
import jax
import jax.numpy as jnp
from jax import lax
from jax.experimental import pallas as pl
from jax.experimental.pallas import tpu as pltpu

F32 = jnp.float32
BF16 = jnp.bfloat16

D_MODEL = 1024
D_FF = 2816
RMS_EPS = 1e-6
POOL_WINDOWS = (2, 4, 8, 16)
POOL_GROUP_DIM = D_MODEL // len(POOL_WINDOWS)
POOL_STATE_LEN = max(POOL_WINDOWS) - 1
ATTN_WINDOWS = (128, 512, 2048)
ATTN_DILATIONS = (1, 4, 16)
N_GROUPS = 3
HEAD_DIM = 64
HEADS = 8
GROUP_WIDTH = HEADS * HEAD_DIM
QKV_WIDTH = N_GROUPS * 3 * GROUP_WIDTH
Q_BLOCK = 128
N_KEYS = Q_BLOCK + 1
ATTN_SCALE = HEAD_DIM ** -0.5
N_BUCKETS = 32
MAX_EXACT = N_BUCKETS // 2
MAX_DISTANCE = 2048
NEG_INF = -1e30

V7X_LANES = 128
V7X_VMEM_BYTES = 64 * 1024 * 1024
V7X_VMEM_RESERVE = 6 * 1024 * 1024
ROW_TILE = 512


def _vmem_limit(estimate_bytes):
    return int(min(V7X_VMEM_BYTES - V7X_VMEM_RESERVE, estimate_bytes))


def _params(n_axes, vmem_bytes):
    return pltpu.CompilerParams(
        dimension_semantics=("arbitrary",) * n_axes,
        vmem_limit_bytes=_vmem_limit(vmem_bytes))


def _resident(shape):
    return pl.BlockSpec(shape, lambda *_: (0,) * len(shape), pipeline_mode=pl.Buffered(1))


def _rms(x, g):
    return x * lax.rsqrt(jnp.mean(x * x, axis=-1, keepdims=True) + RMS_EPS) * g


def _ffn_kernel(x_ref, g_ref, wg_ref, wu_ref, wd_ref, *refs, post):
    pg_ref, out_refs = (None, refs) if post == "none" else (refs[0], refs[1:])
    x = x_ref[...]
    h = _rms(x, g_ref[...]).astype(BF16)
    gate = jnp.dot(h, wg_ref[...], preferred_element_type=F32)
    up = jnp.dot(h, wu_ref[...], preferred_element_type=F32)
    a = (gate * jax.nn.sigmoid(gate) * up).astype(BF16)
    y = x + 0.5 * jnp.dot(a, wd_ref[...], preferred_element_type=F32)
    if post == "none":
        out_refs[0][...] = y
    elif post == "mixer_input":
        out_refs[0][...] = y
        out_refs[1][...] = _rms(y, pg_ref[...]).astype(BF16)
    else:
        out_refs[0][...] = _rms(y, pg_ref[...])


def _layer_resident(shape, layer):
    return pl.BlockSpec((None,) + shape, lambda *_: (layer,) + (0,) * len(shape), pipeline_mode=pl.Buffered(1))


def _ffn(x, g, wg, wu, wd, layer, post_g, post, name):
    rows = x.shape[0]
    tm = min(ROW_TILE, rows)
    row_spec = pl.BlockSpec((tm, D_MODEL), lambda i: (i, 0))
    vec_spec = _resident((1, D_MODEL))
    out_shape = [jax.ShapeDtypeStruct((rows, D_MODEL), F32)]
    out_specs = [row_spec]
    if post == "mixer_input":
        out_shape.append(jax.ShapeDtypeStruct((rows, D_MODEL), BF16))
        out_specs.append(row_spec)
    weights = 3 * D_MODEL * D_FF * 2
    tiles = tm * D_MODEL * (4 * 2 + 4 * 2 + 2 * 2)
    temps = tm * D_FF * (4 + 4 + 4 + 2) + tm * D_MODEL * 12
    post_in = () if post == "none" else (post_g,)
    return pl.pallas_call(
        functools.partial(_ffn_kernel, post=post),
        out_shape=out_shape,
        grid=(rows // tm,),
        in_specs=[row_spec, _layer_resident((1, D_MODEL), layer), _layer_resident((D_MODEL, D_FF), layer),
                  _layer_resident((D_MODEL, D_FF), layer), _layer_resident((D_FF, D_MODEL), layer)]
        + [vec_spec] * len(post_in),
        out_specs=out_specs,
        compiler_params=_params(1, weights + tiles + temps),
        name=name,
    )(x, g, wg, wu, wd, *post_in)


def _pool_tail(x, u, sums_minus_u, counts, wgrp_ref, scale_ref, wout_ref):
    y = x
    for g, w in enumerate(POOL_WINDOWS):
        cols = slice(g * POOL_GROUP_DIM, (g + 1) * POOL_GROUP_DIM)
        ug = u[:, cols]
        z = ((sums_minus_u[g] + ug) / counts[g] - ug).astype(BF16)
        z = jnp.dot(z, wgrp_ref[g], preferred_element_type=F32) * scale_ref[:, cols]
        y = y + jnp.dot(z.astype(BF16), wout_ref[cols, :], preferred_element_type=F32)
    return y


def _pool_prompt_kernel(x_ref, h_ref, win_ref, wgrp_ref, scale_ref, wout_ref,
                        xo_ref, tail_ref, ubuf_ref, *, tm):
    i = pl.program_id(0)
    hist = POOL_STATE_LEN + 1

    @pl.when(i == 0)
    def _():
        ubuf_ref[0:hist, :] = jnp.zeros((hist, D_MODEL), F32)

    u = jnp.dot(h_ref[...], win_ref[...], preferred_element_type=F32)
    ubuf_ref[hist:hist + tm, :] = u
    pos = i * tm + lax.broadcasted_iota(jnp.int32, (tm, 1), 0)
    sums, counts = [], []
    for g, w in enumerate(POOL_WINDOWS):
        cols = slice(g * POOL_GROUP_DIM, (g + 1) * POOL_GROUP_DIM)
        acc = ubuf_ref[hist - 1:hist - 1 + tm, cols]
        for j in range(2, w):
            acc = acc + ubuf_ref[hist - j:hist - j + tm, cols]
        sums.append(acc)
        counts.append(jnp.minimum(w, pos + 1).astype(F32))
    xo_ref[...] = _pool_tail(x_ref[...], u, sums, counts, wgrp_ref, scale_ref, wout_ref)
    last = ubuf_ref[tm:tm + hist, :]
    tail_ref[...] = last
    ubuf_ref[0:hist, :] = last


def _pool_prompt(x, h, win, wgrp, scale, wout):
    rows = x.shape[0]
    tm = ROW_TILE
    hist = POOL_STATE_LEN + 1
    row_spec = pl.BlockSpec((tm, D_MODEL), lambda i: (i, 0))
    weights = (2 * D_MODEL * D_MODEL + 4 * POOL_GROUP_DIM * POOL_GROUP_DIM) * 2
    tiles = tm * D_MODEL * (4 * 2 + 2 * 2 + 4 * 2) + (tm + hist) * D_MODEL * 4
    temps = tm * D_MODEL * 4 * 6
    return pl.pallas_call(
        functools.partial(_pool_prompt_kernel, tm=tm),
        out_shape=[jax.ShapeDtypeStruct((rows, D_MODEL), F32),
                   jax.ShapeDtypeStruct((hist, D_MODEL), F32)],
        grid=(rows // tm,),
        in_specs=[row_spec, row_spec, _resident((D_MODEL, D_MODEL)),
                  _resident((4, POOL_GROUP_DIM, POOL_GROUP_DIM)), _resident((1, D_MODEL)),
                  _resident((D_MODEL, D_MODEL))],
        out_specs=[row_spec, pl.BlockSpec((hist, D_MODEL), lambda i: (0, 0))],
        scratch_shapes=[pltpu.VMEM((tm + hist, D_MODEL), F32)],
        compiler_params=_params(1, weights + tiles + temps),
        name="pool_prompt",
    )(x, h, win, wgrp, scale, wout)


def _pool_step_kernel(x_ref, h_ref, st_ref, win_ref, wgrp_ref, scale_ref, wout_ref,
                      xo_ref, sto_ref, *, past_len):
    u = jnp.dot(h_ref[...], win_ref[...], preferred_element_type=F32)
    sums, counts = [], []
    for g, w in enumerate(POOL_WINDOWS):
        cols = slice(g * POOL_GROUP_DIM, (g + 1) * POOL_GROUP_DIM)
        acc = st_ref[POOL_STATE_LEN - 1, :, cols]
        for j in range(2, w):
            acc = acc + st_ref[POOL_STATE_LEN - j, :, cols]
        sums.append(acc)
        counts.append(float(min(w, past_len + 1)))
    xo_ref[...] = _pool_tail(x_ref[...], u, sums, counts, wgrp_ref, scale_ref, wout_ref)
    for k in range(POOL_STATE_LEN - 1):
        sto_ref[k] = st_ref[k + 1]
    sto_ref[POOL_STATE_LEN - 1] = u


def _pool_step(x, h, state, win, wgrp, scale, wout, past_len):
    rows = x.shape[0]
    full = lambda shape: pl.BlockSpec(shape, lambda i: (0,) * len(shape))
    return pl.pallas_call(
        functools.partial(_pool_step_kernel, past_len=past_len),
        out_shape=[jax.ShapeDtypeStruct((rows, D_MODEL), F32),
                   jax.ShapeDtypeStruct(state.shape, F32)],
        grid=(1,),
        in_specs=[full((rows, D_MODEL)), full((rows, D_MODEL)), full(state.shape),
                  full((D_MODEL, D_MODEL)), full((4, POOL_GROUP_DIM, POOL_GROUP_DIM)),
                  full((1, D_MODEL)), full((D_MODEL, D_MODEL))],
        out_specs=[full((rows, D_MODEL)), full(state.shape)],
        compiler_params=_params(1, 32 * 1024 * 1024),
        name="pool_step",
    )(x, h, state, win, wgrp, scale, wout)


def _qkv_kernel(h_ref, w_ref, *refs, tm, dilations):
    outs, slab_ref = refs[:N_GROUPS], refs[N_GROUPS]
    h = h_ref[...]
    width = 3 * GROUP_WIDTH
    for g, (o_ref, d) in enumerate(zip(outs, dilations)):
        res = jnp.dot(h, w_ref[:, g * width:(g + 1) * width], preferred_element_type=F32)
        if d == 1:
            o_ref[0] = res
            continue
        for c in range(width // V7X_LANES):
            slab_ref[c] = res[:, c * V7X_LANES:(c + 1) * V7X_LANES]
        for r in range(d):
            for c in range(width // V7X_LANES):
                o_ref[r, :, c * V7X_LANES:(c + 1) * V7X_LANES] = slab_ref[c, pl.ds(r, tm // d, stride=d), :]


def _qkv(h, w, dilations, name):
    rows = h.shape[0]
    tm = min(ROW_TILE, rows)
    width = 3 * GROUP_WIDTH
    weights = D_MODEL * QKV_WIDTH * 2
    tiles = tm * D_MODEL * 2 * 2 + tm * QKV_WIDTH * 4 * 2
    temps = tm * width * 4 * 3
    return pl.pallas_call(
        functools.partial(_qkv_kernel, tm=tm, dilations=dilations),
        out_shape=[jax.ShapeDtypeStruct((d, rows // d, width), F32) for d in dilations],
        grid=(rows // tm,),
        in_specs=[pl.BlockSpec((tm, D_MODEL), lambda i: (i, 0)), _resident((D_MODEL, QKV_WIDTH))],
        out_specs=[pl.BlockSpec((d, tm // d, width), lambda i: (0, i, 0)) for d in dilations],
        scratch_shapes=[pltpu.VMEM((width // V7X_LANES, tm, V7X_LANES), F32)],
        compiler_params=_params(1, weights + tiles + temps),
        name=name,
    )(h, w)


def _kv_tail_kernel(h_ref, wk_ref, wv_ref, o_ref):
    h = h_ref[...]
    o_ref[0] = jnp.dot(h, wk_ref[...], preferred_element_type=F32).T
    o_ref[1] = jnp.dot(h, wv_ref[...], preferred_element_type=F32).T


def _kv_tail(h, w, g, window):
    rows = h.shape[0]
    tm = min(window, 256)
    first = (rows - window) // tm
    col = 3 * g
    return pl.pallas_call(
        _kv_tail_kernel,
        out_shape=jax.ShapeDtypeStruct((2, GROUP_WIDTH, window), F32),
        grid=(window // tm,),
        in_specs=[pl.BlockSpec((tm, D_MODEL), lambda i: (first + i, 0)),
                  pl.BlockSpec((D_MODEL, GROUP_WIDTH), lambda i: (0, col + 1)),
                  pl.BlockSpec((D_MODEL, GROUP_WIDTH), lambda i: (0, col + 2))],
        out_specs=pl.BlockSpec((2, GROUP_WIDTH, tm), lambda i: (0, 0, i)),
        compiler_params=_params(1, 16 * 1024 * 1024),
        name="kv_tail_g%d" % g,
    )(h, w, w)


def _bucket_table():
    out = np.zeros((N_GROUPS, N_KEYS), np.int32)
    for g, d in enumerate(ATTN_DILATIONS):
        dist = np.arange(N_KEYS, dtype=np.int32) * d
        distf = np.maximum(dist, 1).astype(np.float32)
        log_b = MAX_EXACT + (np.log(distf / np.float32(MAX_EXACT)) / np.float32(math.log(MAX_DISTANCE / MAX_EXACT))
                             * np.float32(N_BUCKETS - MAX_EXACT)).astype(np.int32)
        log_b = np.minimum(log_b, N_BUCKETS - 1)
        out[g] = np.where(dist < MAX_EXACT, dist, log_b)
    return out


def _band_offsets():
    a = np.arange(Q_BLOCK)[:, None]
    b = np.arange(2 * Q_BLOCK)[None, :]
    j = Q_BLOCK + a - b
    return np.where((j >= 0) & (j <= Q_BLOCK), j, -1).astype(np.int32)


def _bias_kernel(tab_ref, bidx_ref, l0_ref, l1_ref, l2_ref, band_ref, s0_ref, s1_ref, s2_ref, new_ref, *, buckets):
    sub = lax.broadcasted_iota(jnp.int32, (HEADS, V7X_LANES), 0)
    for g, (lidx_ref, step_ref) in enumerate(((l0_ref, s0_ref), (l1_ref, s1_ref), (l2_ref, s2_ref))):
        bidx = bidx_ref[g]
        lidx = lidx_ref[...]
        used = sorted(set(int(v) for v in buckets[g]))
        t = jnp.zeros((HEADS, V7X_LANES), F32)
        for h in range(HEADS):
            t = jnp.where(sub == h, tab_ref[int(buckets[g][0]), g * HEADS + h], t)
        new_ref[g] = t
        for h in range(HEADS):
            tile = jnp.full(bidx.shape, NEG_INF, F32)
            row = jnp.full(lidx.shape, NEG_INF, F32)
            for v in used:
                tile = jnp.where(bidx == v, tab_ref[v, g * HEADS + h], tile)
                row = jnp.where(lidx == v, tab_ref[v, g * HEADS + h], row)
            band_ref[g, h] = tile
            step_ref[h:h + 1, :] = row


def _bias_tables(rel_bias):
    buckets = _bucket_table()
    band = _band_offsets()
    bidx = np.stack([np.where(band >= 0, buckets[g][np.maximum(band, 0)], -1) for g in range(N_GROUPS)])
    lidx = []
    for g, (w, d) in enumerate(zip(ATTN_WINDOWS, ATTN_DILATIONS)):
        pos = np.arange(w)
        lidx.append(np.where(pos % d == 0, buckets[g][(w - pos) // d], -1).astype(np.int32)[None])
    vmem = pl.BlockSpec(memory_space=pltpu.VMEM)
    return pl.pallas_call(
        functools.partial(_bias_kernel, buckets=buckets),
        out_shape=[jax.ShapeDtypeStruct((N_GROUPS, HEADS, Q_BLOCK, 2 * Q_BLOCK), F32)]
        + [jax.ShapeDtypeStruct((HEADS, w), F32) for w in ATTN_WINDOWS]
        + [jax.ShapeDtypeStruct((N_GROUPS, HEADS, V7X_LANES), F32)],
        in_specs=[pl.BlockSpec(memory_space=pltpu.SMEM)] + [vmem] * 4,
        out_specs=[vmem] * 5,
        compiler_params=pltpu.CompilerParams(vmem_limit_bytes=_vmem_limit(24 * 1024 * 1024)),
        name="bias_tables",
    )(rel_bias, jnp.asarray(bidx), *[jnp.asarray(l) for l in lidx])


ATTN_BLOCKS_PER_STEP = 4


def _attn_prompt_kernel(q_ref, kp_ref, kc_ref, vp_ref, vc_ref, bias_ref, o_ref, lse_ref, *, nb):
    low = lax.broadcasted_iota(jnp.int32, (1, V7X_LANES), 1) < HEAD_DIM
    col = lax.broadcasted_iota(jnp.int32, (1, 2 * Q_BLOCK), 1)
    no_prev = jnp.where(jnp.logical_and(pl.program_id(1) == 0, col < Q_BLOCK), NEG_INF, 0.0)
    q = q_ref[...] * ATTN_SCALE
    k = jnp.concatenate([kp_ref[...], kc_ref[...]], axis=0).astype(BF16)
    v = jnp.concatenate([vp_ref[...], vc_ref[...]], axis=0)
    items = [(b, p, half) for b in range(nb) for p in range(HEADS // 2) for half in range(2)]
    q_rows = lambda b: slice(b * Q_BLOCK, (b + 1) * Q_BLOCK)
    k_rows = lambda b: slice(b * Q_BLOCK, (b + 2) * Q_BLOCK)
    lanes = lambda p: slice(p * V7X_LANES, (p + 1) * V7X_LANES)
    mine = lambda half: low if half == 0 else jnp.logical_not(low)

    scores = {}
    for b, p, half in items:
        qh = jnp.where(mine(half), q[q_rows(b), lanes(p)], 0.0).astype(BF16)
        s = lax.dot_general(qh, k[k_rows(b), lanes(p)], (((1,), (1,)), ((), ())), preferred_element_type=F32)
        s = s + bias_ref[2 * p + half]
        scores[b, p, half] = s + no_prev if b == 0 else s
    tops = {it: jnp.max(scores[it], axis=-1, keepdims=True) for it in items}
    weights = {it: jnp.exp(scores[it] - tops[it]).astype(BF16) for it in items}
    acc = {}
    for b, p, half in items:
        vh = jnp.where(mine(half), v[k_rows(b), lanes(p)], 1.0).astype(BF16)
        acc[b, p, half] = jnp.dot(weights[b, p, half], vh, preferred_element_type=F32)
    for b in range(nb):
        for p in range(HEADS // 2):
            a0, a1 = acc[b, p, 0], acc[b, p, 1]
            den = pltpu.roll(jnp.where(low, a1, a0), HEAD_DIM, 1)
            o_ref[q_rows(b), lanes(p)] = jnp.where(low, a0, a1) * (1.0 / den)
            lse_ref[q_rows(b), lanes(p)] = jnp.where(low, tops[b, p, 0], tops[b, p, 1]) + jnp.log(den)


def _attn_prompt(qkv, band_bias, g):
    d, rows, _ = qkv.shape
    nb = ATTN_BLOCKS_PER_STEP
    step_rows = nb * Q_BLOCK

    def spec(which, prev):
        if prev:
            return pl.BlockSpec((None, Q_BLOCK, GROUP_WIDTH), lambda r, n: (r, jnp.maximum(nb * n - 1, 0), which))
        return pl.BlockSpec((None, step_rows, GROUP_WIDTH), lambda r, n: (r, n, which))

    out_spec = pl.BlockSpec((None, step_rows, GROUP_WIDTH), lambda r, n: (r, n, 0))
    bias_spec = pl.BlockSpec((None, HEADS, Q_BLOCK, 2 * Q_BLOCK), lambda r, n: (g, 0, 0, 0))
    return pl.pallas_call(
        functools.partial(_attn_prompt_kernel, nb=nb),
        out_shape=[jax.ShapeDtypeStruct((d, rows, GROUP_WIDTH), F32)] * 2,
        grid=(d, rows // step_rows),
        in_specs=[spec(0, False), spec(1, True), spec(1, False), spec(2, True), spec(2, False), bias_spec],
        out_specs=[out_spec, out_spec],
        compiler_params=_params(2, 32 * 1024 * 1024),
        name="attn_prompt_g%d" % g,
    )(qkv, qkv, qkv, qkv, qkv, band_bias)


def _attn_step_kernel(q_ref, kn_ref, vn_ref, knt_ref, vnt_ref, kc_ref, vc_ref, bias_ref, bnew_ref,
                      o_ref, lse_ref, ko_ref, vo_ref, *, bb, length):
    sub = lax.broadcasted_iota(jnp.int32, (HEADS, 1), 0)
    is_last = lax.broadcasted_iota(jnp.int32, (1, length), 1) == length - 1
    for b in range(bb):
        q = q_ref[b] * ATTN_SCALE
        qb = q.astype(BF16)
        s = jnp.zeros((HEADS, length), F32)
        for h in range(HEADS):
            kh = kc_ref[b, h]
            s = jnp.where(sub == h, jnp.dot(qb, kh.astype(BF16), preferred_element_type=F32), s)
            ko_ref[b, h] = jnp.where(is_last, knt_ref[b, :, h:h + 1], pltpu.roll(kh, length - 1, 1))
        s = s + bias_ref[...]
        s_new = jnp.sum(q * kn_ref[b], axis=-1, keepdims=True) + bnew_ref[:, 0:1]
        m = jnp.maximum(jnp.max(s, axis=-1, keepdims=True), s_new)
        e = jnp.exp(s - m)
        e_new = jnp.exp(s_new - m)
        den = jnp.sum(e, axis=-1, keepdims=True) + e_new
        inv = 1.0 / den
        prob = (e * inv).astype(BF16)
        o = jnp.zeros((HEADS, HEAD_DIM), F32)
        for h in range(HEADS):
            vh = vc_ref[b, h]
            o_all = lax.dot_general(prob, vh.astype(BF16), (((1,), (1,)), ((), ())),
                                    preferred_element_type=F32)
            o = jnp.where(sub == h, o_all, o)
            vo_ref[b, h] = jnp.where(is_last, vnt_ref[b, :, h:h + 1], pltpu.roll(vh, length - 1, 1))
        o_ref[b] = o + (e_new * inv) * vn_ref[b]
        lse_ref[b] = jnp.broadcast_to(m + jnp.log(den), (HEADS, HEAD_DIM))


def _attn_step(q, k_new, v_new, cache_k, cache_v, bias_row, bias_new, g, layer):
    batch = q.shape[0]
    length = cache_k.shape[-1]
    bb = max(1, min(batch, (4 * 1024 * 1024) // (HEADS * HEAD_DIM * length * 4)))
    small = pl.BlockSpec((bb, HEADS, HEAD_DIM), lambda i: (i, 0, 0))
    column = pl.BlockSpec((bb, HEAD_DIM, HEADS), lambda i: (i, 0, 0))
    cache_spec = pl.BlockSpec((None, bb, HEADS, HEAD_DIM, length), lambda i: (layer, i, 0, 0, 0))
    out_cache_spec = pl.BlockSpec((None, bb, HEADS, HEAD_DIM, length), lambda i: (0, i, 0, 0, 0))
    cache_block = bb * HEADS * HEAD_DIM * length * 4
    column_block = bb * HEADS * HEAD_DIM * V7X_LANES * 4
    return pl.pallas_call(
        functools.partial(_attn_step_kernel, bb=bb, length=length),
        out_shape=[jax.ShapeDtypeStruct((batch, HEADS, HEAD_DIM), F32)] * 2
        + [jax.ShapeDtypeStruct((1,) + cache_k.shape[1:], F32)] * 2,
        grid=(batch // bb,),
        in_specs=[small, small, small, column, column, cache_spec, cache_spec,
                  pl.BlockSpec((HEADS, length), lambda i: (0, 0)),
                  pl.BlockSpec((None, HEADS, V7X_LANES), lambda i: (g, 0, 0))],
        out_specs=[small, small, out_cache_spec, out_cache_spec],
        compiler_params=_params(1, 8 * cache_block + 4 * column_block + 8 * 1024 * 1024),
        name="attn_step_g%d" % g,
    )(q, k_new, v_new, jnp.swapaxes(k_new, 1, 2), jnp.swapaxes(v_new, 1, 2), cache_k, cache_v, bias_row, bias_new)


def _merge_kernel(x_ref, *refs, tm, dilations):
    w_ref, xo_ref, slab_ref = refs[2 * N_GROUPS:]
    n_slabs = GROUP_WIDTH // V7X_LANES

    def token_order(ref, d, slot):
        if d == 1:
            return ref[0]
        for r in range(d):
            for c in range(n_slabs):
                slab_ref[slot, c, pl.ds(r, tm // d, stride=d), :] = ref[r, :, c * V7X_LANES:(c + 1) * V7X_LANES]
        return jnp.concatenate([slab_ref[slot, c] for c in range(n_slabs)], axis=1)

    os_ = [token_order(refs[g], d, g) for g, d in enumerate(dilations)]
    ls = [token_order(refs[N_GROUPS + g], d, N_GROUPS + g) for g, d in enumerate(dilations)]
    m = jnp.maximum(jnp.maximum(ls[0], ls[1]), ls[2])
    es = [jnp.exp(l - m) for l in ls]
    inv = 1.0 / (es[0] + es[1] + es[2])
    o = es[0] * inv * os_[0] + es[1] * inv * os_[1] + es[2] * inv * os_[2]
    xo_ref[...] = x_ref[...] + jnp.dot(o.astype(BF16), w_ref[...], preferred_element_type=F32)


def _merge(x, outs, lses, w, dilations, name):
    rows = x.shape[0]
    tm = min(ROW_TILE, rows)
    row_spec = pl.BlockSpec((tm, D_MODEL), lambda i: (i, 0))
    grp_specs = [pl.BlockSpec((d, tm // d, GROUP_WIDTH), lambda i: (0, i, 0)) for d in dilations]
    tiles = tm * D_MODEL * 4 * 4 + tm * GROUP_WIDTH * 4 * (2 * 6 + 6) + GROUP_WIDTH * D_MODEL * 2
    temps = tm * GROUP_WIDTH * 4 * 8
    return pl.pallas_call(
        functools.partial(_merge_kernel, tm=tm, dilations=dilations),
        out_shape=jax.ShapeDtypeStruct((rows, D_MODEL), F32),
        grid=(rows // tm,),
        in_specs=[row_spec] + grp_specs * 2 + [_resident((GROUP_WIDTH, D_MODEL))],
        out_specs=row_spec,
        scratch_shapes=[pltpu.VMEM((2 * N_GROUPS, GROUP_WIDTH // V7X_LANES, tm, V7X_LANES), F32)],
        compiler_params=_params(1, tiles + temps),
        name=name,
    )(x, *outs, *lses, w)


def kernel(x_prompt, x_sample, state_pool, cache_k_w128, cache_v_w128, cache_k_w512, cache_v_w512,
           cache_k_w2048, cache_v_w2048, ffn1_norm, ffn1_w_gate, ffn1_w_up, ffn1_w_down, mix_norm,
           pool_w_in, pool_w_group, pool_scale, pool_w_out, attn_w_qkv, attn_w_out, rel_bias,
           ffn2_norm, ffn2_w_gate, ffn2_w_up, ffn2_w_down, final_norm):
    seq = x_prompt.shape[1]
    batch = x_sample.shape[0]
    past_len = cache_k_w2048.shape[2]
    cache_k = (cache_k_w128, cache_k_w512, cache_k_w2048)
    cache_v = (cache_v_w128, cache_v_w512, cache_v_w2048)
    bf = lambda w: w.astype(BF16)
    vec = lambda v: v.reshape(1, D_MODEL)

    xp = x_prompt.reshape(seq, D_MODEL)
    xs = x_sample.reshape(batch, D_MODEL)
    band_bias, *step_bias, new_bias = _bias_tables(rel_bias)
    to_pos_minor = lambda c: jnp.transpose(c, (0, 1, 3, 4, 2))
    from_pos_minor = lambda c: jnp.transpose(c, (0, 1, 4, 2, 3))

    ffn_params = {1: (ffn1_norm[:, None], bf(ffn1_w_gate), bf(ffn1_w_up), bf(ffn1_w_down)),
                  2: (ffn2_norm[:, None], bf(ffn2_w_gate), bf(ffn2_w_up), bf(ffn2_w_down))}

    def ffn(x, which, layer, post_g, post, name):
        post_g = None if post_g is None else vec(post_g)
        return _ffn(x, *ffn_params[which], layer, post_g, post, name)

    xp, hp = ffn(xp, 1, 0, mix_norm[0], "mixer_input", "ffn1_l0_prompt")
    xs, hs = ffn(xs, 1, 0, mix_norm[0], "mixer_input", "ffn1_l0_step")
    pool_w = (bf(pool_w_in[0]), bf(pool_w_group[0]), vec(pool_scale[0]), bf(pool_w_out[0]))
    xp, tail = _pool_prompt(xp, hp, *pool_w)
    xs, pool_state_s = _pool_step(xs, hs, jnp.swapaxes(state_pool[0], 0, 1), *pool_w, past_len)
    pool_state_p = tail[1:][None, None]
    pool_state_s = jnp.swapaxes(pool_state_s, 0, 1)[None]
    xp, = ffn(xp, 2, 0, None, "none", "ffn2_l0_prompt")
    xs, = ffn(xs, 2, 0, None, "none", "ffn2_l0_step")

    xp, hp = ffn(xp, 1, 1, mix_norm[1], "mixer_input", "ffn1_l1_prompt")
    xs, hs = ffn(xs, 1, 1, mix_norm[1], "mixer_input", "ffn1_l1_step")
    wqkv = bf(attn_w_qkv[0])
    wout = bf(attn_w_out[0])
    no_dilation = (1,) * N_GROUPS
    qkv_p = _qkv(hp, wqkv, ATTN_DILATIONS, "qkv_prompt")
    qkv_s = _qkv(hs, wqkv, no_dilation, "qkv_step")

    outs_p, lses_p, outs_s, lses_s, kv_prompt, rolled = [], [], [], [], [], []
    for g, w in enumerate(ATTN_WINDOWS):
        o, lse = _attn_prompt(qkv_p[g], band_bias, g)
        outs_p.append(o)
        lses_p.append(lse)
        keep = min(w, seq)
        kv = _kv_tail(hp, wqkv, g, keep).reshape(2, 1, 1, HEADS, HEAD_DIM, keep)
        kv_prompt.append((from_pos_minor(kv[0]), from_pos_minor(kv[1])))
        q, k_new, v_new = (qkv_s[g][0, :, t * GROUP_WIDTH:(t + 1) * GROUP_WIDTH].reshape(batch, HEADS, HEAD_DIM)
                           for t in range(3))
        o, lse, k_roll, v_roll = _attn_step(q, k_new, v_new, to_pos_minor(cache_k[g]), to_pos_minor(cache_v[g]),
                                            step_bias[g], new_bias, g, 0)
        outs_s.append(o.reshape(1, batch, GROUP_WIDTH))
        lses_s.append(lse.reshape(1, batch, GROUP_WIDTH))
        rolled += [from_pos_minor(k_roll), from_pos_minor(v_roll)]
    xp = _merge(xp, outs_p, lses_p, wout, ATTN_DILATIONS, "merge_prompt")
    xs = _merge(xs, outs_s, lses_s, wout, no_dilation, "merge_step")

    yp, = ffn(xp, 2, 1, final_norm, "final_norm", "ffn2_l1_prompt")
    ys, = ffn(xs, 2, 1, final_norm, "final_norm", "ffn2_l1_step")

    return (yp.reshape(x_prompt.shape), ys.reshape(x_sample.shape), pool_state_p, pool_state_s,
            kv_prompt[0][0], kv_prompt[0][1], rolled[0], rolled[1],
            kv_prompt[1][0], kv_prompt[1][1], rolled[2], rolled[3],
            kv_prompt[2][0], kv_prompt[2][1], rolled[4], rolled[5])
```

```python
import functools
import math

import numpy as np
import jax
import jax.numpy as jnp
from jax import lax
from jax.experimental import pallas as pl
from jax.experimental.pallas import tpu as pltpu

F32 = jnp.float32
BF16 = jnp.bfloat16

D_MODEL = 1024
D_FF = 2816
RMS_EPS = 1e-6
POOL_WINDOWS = (2, 4, 8, 16)
POOL_GROUP_DIM = D_MODEL // len(POOL_WINDOWS)
POOL_STATE_LEN = max(POOL_WINDOWS) - 1
ATTN_WINDOWS = (128, 512, 2048)
ATTN_DILATIONS = (1, 4, 16)
N_GROUPS = 3
HEAD_DIM = 64
HEADS = 8
GROUP_WIDTH = HEADS * HEAD_DIM
QKV_WIDTH = N_GROUPS * 3 * GROUP_WIDTH
Q_BLOCK = 128
N_KEYS = Q_BLOCK + 1
ATTN_SCALE = HEAD_DIM ** -0.5
N_BUCKETS = 32
MAX_EXACT = N_BUCKETS // 2
MAX_DISTANCE = 2048
NEG_INF = -1e30

V7X_LANES = 128
V7X_VMEM_BYTES = 64 * 1024 * 1024
V7X_VMEM_RESERVE = 6 * 1024 * 1024
ROW_TILE = 512


def _vmem_limit(estimate_bytes):
    return int(min(V7X_VMEM_BYTES - V7X_VMEM_RESERVE, estimate_bytes))


def _params(n_axes, vmem_bytes):
    return pltpu.CompilerParams(
        dimension_semantics=("arbitrary",) * n_axes,
        vmem_limit_bytes=_vmem_limit(vmem_bytes))


def _resident(shape):
    return pl.BlockSpec(shape, lambda *_: (0,) * len(shape), pipeline_mode=pl.Buffered(1))


def _rms(x, g):
    return x * lax.rsqrt(jnp.mean(x * x, axis=-1, keepdims=True) + RMS_EPS) * g


def _ffn_kernel(x_ref, g_ref, wg_ref, wu_ref, wd_ref, *refs, post, riders):
    n_post = 0 if post == "none" else 1
    n_out = 2 if post == "mixer_input" else 1
    n_rider_in, n_rider_out = 7 * riders, 4 * riders
    pg_ref = refs[0] if n_post else None
    rider_in = refs[n_post:n_post + n_rider_in]
    out_refs = refs[n_post + n_rider_in:n_post + n_rider_in + n_out]
    rider_out = refs[n_post + n_rider_in + n_out:]
    assert len(rider_out) == n_rider_out
    for r in range(riders):
        _decode_attn_item(*rider_in[7 * r:7 * r + 7], *rider_out[4 * r:4 * r + 4])
    x = x_ref[...]
    h = _rms(x, g_ref[...]).astype(BF16)
    gate = jnp.dot(h, wg_ref[...], preferred_element_type=F32)
    up = jnp.dot(h, wu_ref[...], preferred_element_type=F32)
    a = (gate * jax.nn.sigmoid(gate) * up).astype(BF16)
    y = x + 0.5 * jnp.dot(a, wd_ref[...], preferred_element_type=F32)
    if post == "none":
        out_refs[0][...] = y
    elif post == "mixer_input":
        out_refs[0][...] = y
        out_refs[1][...] = _rms(y, pg_ref[...]).astype(BF16)
    else:
        out_refs[0][...] = _rms(y, pg_ref[...])


def _layer_resident(shape, layer):
    return pl.BlockSpec((None,) + shape, lambda *_: (layer,) + (0,) * len(shape), pipeline_mode=pl.Buffered(1))


def _ffn(x, g, wg, wu, wd, layer, post_g, post, name, riders=()):
    rows = x.shape[0]
    tm = min(ROW_TILE, rows)
    rider_in, rider_in_specs, rider_out_shapes, rider_out_specs = [], [], [], []
    rider_bytes = 0
    for group, cache_layer, operands in riders:
        cache_k = operands[3]
        batch, quads = cache_k.shape[1], cache_k.shape[2]
        tm = rows // (batch * quads)
        in_specs, out_specs = _decode_attn_specs(cache_k, group, cache_layer, lambda i: (i // quads, i % quads))
        rider_in += list(operands)
        rider_in_specs += in_specs
        rider_out_shapes += _decode_attn_out_shapes(cache_k)
        rider_out_specs += out_specs
        rider_bytes += 10 * STEP_HEADS * HEAD_DIM * cache_k.shape[-1] * 4
    row_spec = pl.BlockSpec((tm, D_MODEL), lambda i: (i, 0))
    vec_spec = _resident((1, D_MODEL))
    out_shape = [jax.ShapeDtypeStruct((rows, D_MODEL), F32)]
    out_specs = [row_spec]
    if post == "mixer_input":
        out_shape.append(jax.ShapeDtypeStruct((rows, D_MODEL), BF16))
        out_specs.append(row_spec)
    weights = 3 * D_MODEL * D_FF * 2
    tiles = tm * D_MODEL * (4 * 2 + 4 * 2 + 2 * 2)
    temps = tm * D_FF * (4 + 4 + 4 + 2) + tm * D_MODEL * 12
    post_in = () if post == "none" else (post_g,)
    outs = pl.pallas_call(
        functools.partial(_ffn_kernel, post=post, riders=len(riders)),
        out_shape=out_shape + rider_out_shapes,
        grid=(rows // tm,),
        in_specs=[row_spec, _layer_resident((1, D_MODEL), layer), _layer_resident((D_MODEL, D_FF), layer),
                  _layer_resident((D_MODEL, D_FF), layer), _layer_resident((D_FF, D_MODEL), layer)]
        + [vec_spec] * len(post_in) + rider_in_specs,
        out_specs=out_specs + rider_out_specs,
        compiler_params=_params(1, weights + tiles + temps + rider_bytes),
        name=name,
    )(x, g, wg, wu, wd, *post_in, *rider_in)
    n_own = len(out_shape)
    return outs[:n_own], [_decode_attn_results(*outs[n_own + 4 * r:n_own + 4 * r + 4]) for r in range(len(riders))]


def _pool_tail(x, u, sums, counts, wgrp_ref, scale_ref, wout_ref):
    y = x
    for g, w in enumerate(POOL_WINDOWS):
        cols = slice(g * POOL_GROUP_DIM, (g + 1) * POOL_GROUP_DIM)
        z = (sums[g] / counts[g] - u[:, cols]).astype(BF16)
        z = jnp.dot(z, wgrp_ref[g], preferred_element_type=F32) * scale_ref[:, cols]
        y = y + jnp.dot(z.astype(BF16), wout_ref[cols, :], preferred_element_type=F32)
    return y


def _pool_prompt_kernel(x_ref, h_ref, win_ref, wgrp_ref, scale_ref, wout_ref,
                        xo_ref, tail_ref, hist_ref, *, tm):
    i = pl.program_id(0)
    hist = POOL_STATE_LEN + 1

    @pl.when(i == 0)
    def _():
        hist_ref[...] = jnp.zeros((hist, D_MODEL), F32)

    u = jnp.dot(h_ref[...], win_ref[...], preferred_element_type=F32)
    pos = i * tm + lax.broadcasted_iota(jnp.int32, (tm, 1), 0)
    sums, counts = [], []
    for g, w in enumerate(POOL_WINDOWS):
        cols = slice(g * POOL_GROUP_DIM, (g + 1) * POOL_GROUP_DIM)
        s = jnp.concatenate([hist_ref[:, cols], u[:, cols]], axis=0)
        k = 1
        while k < w:
            s = s + pltpu.roll(s, k, 0)
            k *= 2
        sums.append(s[hist:, :])
        counts.append(jnp.minimum(w, pos + 1).astype(F32))
    xo_ref[...] = _pool_tail(x_ref[...], u, sums, counts, wgrp_ref, scale_ref, wout_ref)
    last = u[tm - hist:, :]
    tail_ref[...] = last
    hist_ref[...] = last


def _pool_prompt(x, h, win, wgrp, scale, wout):
    rows = x.shape[0]
    tm = ROW_TILE
    hist = POOL_STATE_LEN + 1
    row_spec = pl.BlockSpec((tm, D_MODEL), lambda i: (i, 0))
    weights = (2 * D_MODEL * D_MODEL + 4 * POOL_GROUP_DIM * POOL_GROUP_DIM) * 2
    tiles = tm * D_MODEL * (4 * 2 + 2 * 2 + 4 * 2) + (tm + hist) * D_MODEL * 4
    temps = tm * D_MODEL * 4 * 6
    return pl.pallas_call(
        functools.partial(_pool_prompt_kernel, tm=tm),
        out_shape=[jax.ShapeDtypeStruct((rows, D_MODEL), F32),
                   jax.ShapeDtypeStruct((hist, D_MODEL), F32)],
        grid=(rows // tm,),
        in_specs=[row_spec, row_spec, _resident((D_MODEL, D_MODEL)),
                  _resident((4, POOL_GROUP_DIM, POOL_GROUP_DIM)), _resident((1, D_MODEL)),
                  _resident((D_MODEL, D_MODEL))],
        out_specs=[row_spec, pl.BlockSpec((hist, D_MODEL), lambda i: (0, 0))],
        scratch_shapes=[pltpu.VMEM((hist, D_MODEL), F32)],
        compiler_params=_params(1, weights + tiles + temps),
        name="pool_prompt",
    )(x, h, win, wgrp, scale, wout)


def _pool_step_kernel(x_ref, h_ref, st_ref, win_ref, wgrp_ref, scale_ref, wout_ref,
                      xo_ref, sto_ref, *, past_len):
    u = jnp.dot(h_ref[...], win_ref[...], preferred_element_type=F32)
    sums, counts = [], []
    for g, w in enumerate(POOL_WINDOWS):
        cols = slice(g * POOL_GROUP_DIM, (g + 1) * POOL_GROUP_DIM)
        acc = u[:, cols]
        for j in range(1, w):
            acc = acc + st_ref[POOL_STATE_LEN - j, :, cols]
        sums.append(acc)
        counts.append(float(min(w, past_len + 1)))
    xo_ref[...] = _pool_tail(x_ref[...], u, sums, counts, wgrp_ref, scale_ref, wout_ref)
    for k in range(POOL_STATE_LEN - 1):
        sto_ref[k] = st_ref[k + 1]
    sto_ref[POOL_STATE_LEN - 1] = u


def _pool_step(x, h, state, win, wgrp, scale, wout, past_len):
    rows = x.shape[0]
    full = lambda shape: pl.BlockSpec(shape, lambda i: (0,) * len(shape))
    return pl.pallas_call(
        functools.partial(_pool_step_kernel, past_len=past_len),
        out_shape=[jax.ShapeDtypeStruct((rows, D_MODEL), F32),
                   jax.ShapeDtypeStruct(state.shape, F32)],
        grid=(1,),
        in_specs=[full((rows, D_MODEL)), full((rows, D_MODEL)), full(state.shape),
                  full((D_MODEL, D_MODEL)), full((4, POOL_GROUP_DIM, POOL_GROUP_DIM)),
                  full((1, D_MODEL)), full((D_MODEL, D_MODEL))],
        out_specs=[full((rows, D_MODEL)), full(state.shape)],
        compiler_params=_params(1, 32 * 1024 * 1024),
        name="pool_step",
    )(x, h, state, win, wgrp, scale, wout)


def _qkv_kernel(*refs, tm, dilations, q_scale):
    n_slabs = D_MODEL // V7X_LANES
    x_refs, (g_ref, w_ref) = refs[:n_slabs], refs[n_slabs:n_slabs + 2]
    outs, h_ref = refs[n_slabs + 2:-1], refs[-1]
    width = 3 * GROUP_WIDTH
    slot = {}
    for d in sorted(set(dilations), reverse=True):
        if d == 1:
            x = jnp.concatenate([x_ref[...] for x_ref in x_refs], axis=1)
        else:
            x = jnp.concatenate(
                [jnp.concatenate([x_ref[pl.ds(r, tm // d, stride=d), :] for x_ref in x_refs], axis=1)
                 for r in range(d)], axis=0)
        slot[d] = len(slot)
        h_ref[slot[d]] = _rms(x, g_ref[...]).astype(BF16)
    for g, (o_ref, d) in enumerate(zip(outs, dilations)):
        res = jnp.dot(h_ref[slot[d]], w_ref[:, g * width:(g + 1) * width], preferred_element_type=F32)
        for r in range(d):
            rows = slice(r * (tm // d), (r + 1) * (tm // d))
            o_ref[r, :, 0:GROUP_WIDTH] = (res[rows, 0:GROUP_WIDTH] * q_scale).astype(o_ref.dtype)
            o_ref[r, :, GROUP_WIDTH:] = res[rows, GROUP_WIDTH:].astype(o_ref.dtype)


def _qkv(x, gain, w, dilations, q_scale, dtype, name):
    rows = x.shape[0]
    tm = min(ROW_TILE, rows)
    n_slabs = D_MODEL // V7X_LANES
    width = 3 * GROUP_WIDTH
    weights = D_MODEL * QKV_WIDTH * 2
    tiles = tm * D_MODEL * 4 * 2 + tm * QKV_WIDTH * jnp.dtype(dtype).itemsize * 2
    temps = tm * D_MODEL * 10 + tm * width * 4 * 2
    slab_specs = [pl.BlockSpec((tm, V7X_LANES), functools.partial(lambda i, c: (i, c), c=c)) for c in range(n_slabs)]
    return pl.pallas_call(
        functools.partial(_qkv_kernel, tm=tm, dilations=dilations, q_scale=q_scale),
        out_shape=[jax.ShapeDtypeStruct((d, rows // d, width), dtype) for d in dilations],
        grid=(rows // tm,),
        in_specs=slab_specs + [_resident((1, D_MODEL)), _resident((D_MODEL, QKV_WIDTH))],
        out_specs=[pl.BlockSpec((d, tm // d, width), lambda i: (0, i, 0)) for d in dilations],
        scratch_shapes=[pltpu.VMEM((len(set(dilations)), tm, D_MODEL), BF16)],
        compiler_params=_params(1, weights + tiles + temps),
        name=name,
    )(*([x] * n_slabs), gain, w)


def _kv_tail_kernel(x_ref, g_ref, wk_ref, wv_ref, o_ref):
    h = _rms(x_ref[...], g_ref[...]).astype(BF16)
    o_ref[0] = jnp.dot(h, wk_ref[...], preferred_element_type=F32).T
    o_ref[1] = jnp.dot(h, wv_ref[...], preferred_element_type=F32).T


def _kv_tail(x, gain, w, g, window):
    rows = x.shape[0]
    tm = min(window, 256)
    first = (rows - window) // tm
    col = 3 * g
    return pl.pallas_call(
        _kv_tail_kernel,
        out_shape=jax.ShapeDtypeStruct((2, GROUP_WIDTH, window), F32),
        grid=(window // tm,),
        in_specs=[pl.BlockSpec((tm, D_MODEL), lambda i: (first + i, 0)),
                  pl.BlockSpec((1, D_MODEL), lambda i: (0, 0)),
                  pl.BlockSpec((D_MODEL, GROUP_WIDTH), lambda i: (0, col + 1)),
                  pl.BlockSpec((D_MODEL, GROUP_WIDTH), lambda i: (0, col + 2))],
        out_specs=pl.BlockSpec((2, GROUP_WIDTH, tm), lambda i: (0, 0, i)),
        compiler_params=_params(1, 16 * 1024 * 1024),
        name="kv_tail_g%d" % g,
    )(x, gain, w, w)


def _bucket_table():
    out = np.zeros((N_GROUPS, N_KEYS), np.int32)
    for g, d in enumerate(ATTN_DILATIONS):
        dist = np.arange(N_KEYS, dtype=np.int32) * d
        distf = np.maximum(dist, 1).astype(np.float32)
        log_b = MAX_EXACT + (np.log(distf / np.float32(MAX_EXACT)) / np.float32(math.log(MAX_DISTANCE / MAX_EXACT))
                             * np.float32(N_BUCKETS - MAX_EXACT)).astype(np.int32)
        log_b = np.minimum(log_b, N_BUCKETS - 1)
        out[g] = np.where(dist < MAX_EXACT, dist, log_b)
    return out


def _band_offsets():
    a = np.arange(Q_BLOCK)[:, None]
    b = np.arange(2 * Q_BLOCK)[None, :]
    j = Q_BLOCK + a - b
    return np.where((j >= 0) & (j <= Q_BLOCK), j, -1).astype(np.int32)


def _bias_kernel(tab_ref, bidx_ref, l0_ref, l1_ref, l2_ref, band_ref, s0_ref, s1_ref, s2_ref, new_ref, *, buckets):
    sub = lax.broadcasted_iota(jnp.int32, (HEADS, V7X_LANES), 0)
    for g, (lidx_ref, step_ref) in enumerate(((l0_ref, s0_ref), (l1_ref, s1_ref), (l2_ref, s2_ref))):
        bidx = bidx_ref[g]
        lidx = lidx_ref[...]
        used = sorted(set(int(v) for v in buckets[g]))
        t = jnp.zeros((HEADS, V7X_LANES), F32)
        for h in range(HEADS):
            t = jnp.where(sub == h, tab_ref[int(buckets[g][0]), g * HEADS + h], t)
        new_ref[g] = t
        for h in range(HEADS):
            tile = jnp.full(bidx.shape, NEG_INF, F32)
            row = jnp.full(lidx.shape, NEG_INF, F32)
            for v in used:
                tile = jnp.where(bidx == v, tab_ref[v, g * HEADS + h], tile)
                row = jnp.where(lidx == v, tab_ref[v, g * HEADS + h], row)
            band_ref[g, h] = tile
            step_ref[h:h + 1, :] = row


def _bias_tables(rel_bias):
    buckets = _bucket_table()
    band = _band_offsets()
    bidx = np.stack([np.where(band >= 0, buckets[g][np.maximum(band, 0)], -1) for g in range(N_GROUPS)])
    lidx = []
    for g, (w, d) in enumerate(zip(ATTN_WINDOWS, ATTN_DILATIONS)):
        pos = np.arange(w)
        lidx.append(np.where(pos % d == 0, buckets[g][(w - pos) // d], -1).astype(np.int32)[None])
    vmem = pl.BlockSpec(memory_space=pltpu.VMEM)
    return pl.pallas_call(
        functools.partial(_bias_kernel, buckets=buckets),
        out_shape=[jax.ShapeDtypeStruct((N_GROUPS, HEADS, Q_BLOCK, 2 * Q_BLOCK), F32)]
        + [jax.ShapeDtypeStruct((HEADS, w), F32) for w in ATTN_WINDOWS]
        + [jax.ShapeDtypeStruct((N_GROUPS, HEADS, V7X_LANES), F32)],
        in_specs=[pl.BlockSpec(memory_space=pltpu.SMEM)] + [vmem] * 4,
        out_specs=[vmem] * 5,
        compiler_params=pltpu.CompilerParams(vmem_limit_bytes=_vmem_limit(24 * 1024 * 1024)),
        name="bias_tables",
    )(rel_bias, jnp.asarray(bidx), *[jnp.asarray(l) for l in lidx])


ATTN_BLOCKS_PER_STEP = 4


def _attn_prompt_kernel(q_ref, kp_ref, kc_ref, vp_ref, vc_ref, bias_ref, o_ref, lse_ref, *, nb):
    low = lax.broadcasted_iota(jnp.int32, (1, V7X_LANES), 1) < HEAD_DIM
    col = lax.broadcasted_iota(jnp.int32, (1, 2 * Q_BLOCK), 1)
    no_prev = jnp.where(jnp.logical_and(pl.program_id(1) == 0, col < Q_BLOCK), NEG_INF, 0.0)
    q = q_ref[...]
    k = jnp.concatenate([kp_ref[...], kc_ref[...]], axis=0)
    v = jnp.concatenate([vp_ref[...], vc_ref[...]], axis=0)
    items = [(b, p, half) for b in range(nb) for p in range(HEADS // 2) for half in range(2)]
    q_rows = lambda b: slice(b * Q_BLOCK, (b + 1) * Q_BLOCK)
    k_rows = lambda b: slice(b * Q_BLOCK, (b + 2) * Q_BLOCK)
    lanes = lambda p: slice(p * V7X_LANES, (p + 1) * V7X_LANES)
    mine = lambda half: low if half == 0 else jnp.logical_not(low)

    scores = {}
    for b, p, half in items:
        qh = jnp.where(mine(half), q[q_rows(b), lanes(p)], jnp.zeros((), BF16))
        s = lax.dot_general(qh, k[k_rows(b), lanes(p)], (((1,), (1,)), ((), ())), preferred_element_type=F32)
        s = s + bias_ref[2 * p + half]
        scores[b, p, half] = s + no_prev if b == 0 else s
    tops = {it: jnp.max(scores[it], axis=-1, keepdims=True) for it in items}
    weights = {it: jnp.exp(scores[it] - tops[it]).astype(BF16) for it in items}
    acc = {}
    for b, p, half in items:
        vh = jnp.where(mine(half), v[k_rows(b), lanes(p)], jnp.ones((), BF16))
        acc[b, p, half] = jnp.dot(weights[b, p, half], vh, preferred_element_type=F32)
    for b in range(nb):
        for p in range(HEADS // 2):
            a0, a1 = acc[b, p, 0], acc[b, p, 1]
            den = pltpu.roll(jnp.where(low, a1, a0), HEAD_DIM, 1)
            o_ref[q_rows(b), lanes(p)] = jnp.where(low, a0, a1) * (1.0 / den)
            lse_ref[q_rows(b), lanes(p)] = jnp.where(low, tops[b, p, 0], tops[b, p, 1]) + jnp.log(den)


def _attn_prompt(qkv, band_bias, g):
    d, rows, _ = qkv.shape
    nb = ATTN_BLOCKS_PER_STEP
    step_rows = nb * Q_BLOCK

    def spec(which, prev):
        if prev:
            return pl.BlockSpec((None, Q_BLOCK, GROUP_WIDTH), lambda r, n: (r, jnp.maximum(nb * n - 1, 0), which))
        return pl.BlockSpec((None, step_rows, GROUP_WIDTH), lambda r, n: (r, n, which))

    out_spec = pl.BlockSpec((None, step_rows, GROUP_WIDTH), lambda r, n: (r, n, 0))
    bias_spec = pl.BlockSpec((None, HEADS, Q_BLOCK, 2 * Q_BLOCK), lambda r, n: (g, 0, 0, 0))
    return pl.pallas_call(
        functools.partial(_attn_prompt_kernel, nb=nb),
        out_shape=[jax.ShapeDtypeStruct((d, rows, GROUP_WIDTH), F32)] * 2,
        grid=(d, rows // step_rows),
        in_specs=[spec(0, False), spec(1, True), spec(1, False), spec(2, True), spec(2, False), bias_spec],
        out_specs=[out_spec, out_spec],
        compiler_params=_params(2, 32 * 1024 * 1024),
        name="attn_prompt_g%d" % g,
    )(qkv, qkv, qkv, qkv, qkv, band_bias)


STEP_HEADS = 4
STEP_ITEM_REFS = 11


def _decode_attn_item(q_ref, kn_ref, vn_ref, kc_ref, vc_ref, bias_ref, bnew_ref,
                      o_ref, lse_ref, ko_ref, vo_ref):
    length = kc_ref.shape[-1]
    last_tile = slice(length - V7X_LANES, length)
    is_last = lax.broadcasted_iota(jnp.int32, (1, V7X_LANES), 1) == V7X_LANES - 1

    def rolled(ref, hh, new_col):
        x = pltpu.roll(ref[hh], length - 1, 1)
        return x[:, :length - V7X_LANES], jnp.where(is_last, new_col, x[:, last_tile])

    for hh in range(STEP_HEADS):
        col = slice(hh, hh + 1)
        q = q_ref[:, col] * ATTN_SCALE
        k_new, v_new = kn_ref[:, col], vn_ref[:, col]
        s = jnp.sum(q * kc_ref[hh], axis=0, keepdims=True) + bias_ref[col, :]
        s_new = jnp.sum(q * k_new, axis=0, keepdims=True) + bnew_ref[col, 0:1]
        m = jnp.maximum(jnp.max(s, axis=-1, keepdims=True), s_new)
        e = jnp.exp(s - m)
        e_new = jnp.exp(s_new - m)
        den = jnp.sum(e, axis=-1, keepdims=True) + e_new
        acc = jnp.sum(e * vc_ref[hh], axis=-1, keepdims=True) + e_new * v_new
        o_ref[:, col] = acc * (1.0 / den)
        lse_ref[:, col] = jnp.broadcast_to(m + jnp.log(den), (HEAD_DIM, 1))
        for cache_ref, out_ref, new_col in ((kc_ref, ko_ref, k_new), (vc_ref, vo_ref, v_new)):
            body, tail = rolled(cache_ref, hh, new_col)
            if length > V7X_LANES:
                out_ref[hh, :, :length - V7X_LANES] = body
            out_ref[hh, :, last_tile] = tail


def _decode_attn_specs(cache_k, g, layer, item_of_step):
    length = cache_k.shape[-1]
    column = pl.BlockSpec((None, None, HEAD_DIM, STEP_HEADS), lambda *idx: item_of_step(*idx) + (0, 0))
    cache_in = pl.BlockSpec((None, None, None, STEP_HEADS, HEAD_DIM, length),
                            lambda *idx: (layer,) + item_of_step(*idx) + (0, 0, 0))
    cache_out = pl.BlockSpec((None, None, None, STEP_HEADS, HEAD_DIM, length),
                             lambda *idx: (0,) + item_of_step(*idx) + (0, 0, 0))
    bias = pl.BlockSpec((None, STEP_HEADS, length), lambda *idx: (item_of_step(*idx)[1], 0, 0))
    bnew = pl.BlockSpec((None, None, STEP_HEADS, V7X_LANES), lambda *idx: (g, item_of_step(*idx)[1], 0, 0))
    return [column, column, column, cache_in, cache_in, bias, bnew], [column, column, cache_out, cache_out]


def _decode_attn_operands(q, k_new, v_new, cache_k, cache_v, bias_row, bias_new):
    batch = q.shape[0]
    quads = HEADS // STEP_HEADS
    cols = lambda a: jnp.swapaxes(a.reshape(batch, quads, STEP_HEADS, HEAD_DIM), 2, 3)
    quad = lambda c: c.reshape(c.shape[:2] + (quads, STEP_HEADS) + c.shape[3:])
    return (cols(q), cols(k_new), cols(v_new), quad(cache_k), quad(cache_v),
            bias_row.reshape(quads, STEP_HEADS, -1), bias_new.reshape(N_GROUPS, quads, STEP_HEADS, V7X_LANES))


def _decode_attn_out_shapes(cache_quads):
    _, batch, quads, _, dim, length = cache_quads.shape
    column = jax.ShapeDtypeStruct((batch, quads, dim, STEP_HEADS), F32)
    cache = jax.ShapeDtypeStruct((1, batch, quads, STEP_HEADS, dim, length), F32)
    return [column, column, cache, cache]


def _decode_attn_results(o, lse, k_roll, v_roll):
    batch = o.shape[0]
    rows = lambda a: jnp.swapaxes(a, 2, 3).reshape(1, batch, GROUP_WIDTH)
    whole = lambda c: c.reshape(c.shape[:2] + (HEADS,) + c.shape[4:])
    return rows(o), rows(lse), whole(k_roll), whole(v_roll)


def _attn_step_kernel(*refs):
    _decode_attn_item(*refs)


def _attn_step(q, k_new, v_new, cache_k, cache_v, bias_row, bias_new, g, layer):
    batch = q.shape[0]
    length = cache_k.shape[-1]
    operands = _decode_attn_operands(q, k_new, v_new, cache_k, cache_v, bias_row, bias_new)
    in_specs, out_specs = _decode_attn_specs(operands[3], g, layer, lambda b, hq: (b, hq))
    cache_block = STEP_HEADS * HEAD_DIM * length * 4
    outs = pl.pallas_call(
        _attn_step_kernel,
        out_shape=_decode_attn_out_shapes(operands[3]),
        grid=(batch, HEADS // STEP_HEADS),
        in_specs=in_specs,
        out_specs=out_specs,
        compiler_params=_params(2, 10 * cache_block + 8 * 1024 * 1024),
        name="attn_step_g%d" % g,
    )(*operands)
    return _decode_attn_results(*outs)


def _merge_kernel(x_ref, *refs, tm, dilations):
    w_ref, xo_ref, slab_ref = refs[2 * N_GROUPS:]
    n_slabs = GROUP_WIDTH // V7X_LANES

    def token_order(ref, d, slot):
        if d == 1:
            return ref[0]
        for r in range(d):
            for c in range(n_slabs):
                slab_ref[slot, c, pl.ds(r, tm // d, stride=d), :] = ref[r, :, c * V7X_LANES:(c + 1) * V7X_LANES]
        return jnp.concatenate([slab_ref[slot, c] for c in range(n_slabs)], axis=1)

    os_ = [token_order(refs[g], d, g) for g, d in enumerate(dilations)]
    ls = [token_order(refs[N_GROUPS + g], d, N_GROUPS + g) for g, d in enumerate(dilations)]
    m = jnp.maximum(jnp.maximum(ls[0], ls[1]), ls[2])
    es = [jnp.exp(l - m) for l in ls]
    inv = 1.0 / (es[0] + es[1] + es[2])
    o = es[0] * inv * os_[0] + es[1] * inv * os_[1] + es[2] * inv * os_[2]
    xo_ref[...] = x_ref[...] + jnp.dot(o.astype(BF16), w_ref[...], preferred_element_type=F32)


def _merge(x, outs, lses, w, dilations, name):
    rows = x.shape[0]
    tm = min(ROW_TILE, rows)
    row_spec = pl.BlockSpec((tm, D_MODEL), lambda i: (i, 0))
    grp_specs = [pl.BlockSpec((d, tm // d, GROUP_WIDTH), lambda i: (0, i, 0)) for d in dilations]
    tiles = tm * D_MODEL * 4 * 4 + tm * GROUP_WIDTH * 4 * (2 * 6 + 6) + GROUP_WIDTH * D_MODEL * 2
    temps = tm * GROUP_WIDTH * 4 * 8
    return pl.pallas_call(
        functools.partial(_merge_kernel, tm=tm, dilations=dilations),
        out_shape=jax.ShapeDtypeStruct((rows, D_MODEL), F32),
        grid=(rows // tm,),
        in_specs=[row_spec] + grp_specs * 2 + [_resident((GROUP_WIDTH, D_MODEL))],
        out_specs=row_spec,
        scratch_shapes=[pltpu.VMEM((2 * N_GROUPS, GROUP_WIDTH // V7X_LANES, tm, V7X_LANES), F32)],
        compiler_params=_params(1, tiles + temps),
        name=name,
    )(x, *outs, *lses, w)


def kernel(x_prompt, x_sample, state_pool, cache_k_w128, cache_v_w128, cache_k_w512, cache_v_w512,
           cache_k_w2048, cache_v_w2048, ffn1_norm, ffn1_w_gate, ffn1_w_up, ffn1_w_down, mix_norm,
           pool_w_in, pool_w_group, pool_scale, pool_w_out, attn_w_qkv, attn_w_out, rel_bias,
           ffn2_norm, ffn2_w_gate, ffn2_w_up, ffn2_w_down, final_norm):
    seq = x_prompt.shape[1]
    batch = x_sample.shape[0]
    past_len = cache_k_w2048.shape[2]
    cache_k = (cache_k_w128, cache_k_w512, cache_k_w2048)
    cache_v = (cache_v_w128, cache_v_w512, cache_v_w2048)
    bf = lambda w: w.astype(BF16)
    vec = lambda v: v.reshape(1, D_MODEL)

    xp = x_prompt.reshape(seq, D_MODEL)
    xs = x_sample.reshape(batch, D_MODEL)
    band_bias, *step_bias, new_bias = _bias_tables(rel_bias)
    to_pos_minor = lambda c: jnp.transpose(c, (0, 1, 3, 4, 2))
    from_pos_minor = lambda c: jnp.transpose(c, (0, 1, 4, 2, 3))

    ffn_params = {1: (ffn1_norm[:, None], bf(ffn1_w_gate), bf(ffn1_w_up), bf(ffn1_w_down)),
                  2: (ffn2_norm[:, None], bf(ffn2_w_gate), bf(ffn2_w_up), bf(ffn2_w_down))}

    def ffn(x, which, layer, post_g, post, name, riders=()):
        post_g = None if post_g is None else vec(post_g)
        return _ffn(x, *ffn_params[which], layer, post_g, post, name, riders)

    pool_w = (bf(pool_w_in[0]), bf(pool_w_group[0]), vec(pool_scale[0]), bf(pool_w_out[0]))
    wqkv = bf(attn_w_qkv[0])
    wout = bf(attn_w_out[0])
    attn_gain = vec(mix_norm[1])
    no_dilation = (1,) * N_GROUPS

    (xs, hs), _ = ffn(xs, 1, 0, mix_norm[0], "mixer_input", "ffn1_l0_step")
    xs, pool_state_s = _pool_step(xs, hs, jnp.swapaxes(state_pool[0], 0, 1), *pool_w, past_len)
    pool_state_s = jnp.swapaxes(pool_state_s, 0, 1)[None]
    (xs,), _ = ffn(xs, 2, 0, None, "none", "ffn2_l0_step")
    (xs,), _ = ffn(xs, 1, 1, None, "none", "ffn1_l1_step")
    qkv_s = _qkv(xs, attn_gain, wqkv, no_dilation, 1.0, F32, "qkv_step")
    decode_jobs = []
    for g in range(N_GROUPS):
        q, k_new, v_new = (qkv_s[g][0, :, t * GROUP_WIDTH:(t + 1) * GROUP_WIDTH].reshape(batch, HEADS, HEAD_DIM)
                           for t in range(3))
        decode_jobs.append((g, 0, _decode_attn_operands(q, k_new, v_new, to_pos_minor(cache_k[g]),
                                                        to_pos_minor(cache_v[g]), step_bias[g], new_bias)))

    (xp, hp), decoded = ffn(xp, 1, 0, mix_norm[0], "mixer_input", "ffn1_l0_prompt", decode_jobs)
    xp, tail = _pool_prompt(xp, hp, *pool_w)
    pool_state_p = tail[1:][None, None]
    (xp,), _ = ffn(xp, 2, 0, None, "none", "ffn2_l0_prompt")

    (xp,), _ = ffn(xp, 1, 1, None, "none", "ffn1_l1_prompt")
    qkv_p = _qkv(xp, attn_gain, wqkv, ATTN_DILATIONS, ATTN_SCALE, BF16, "qkv_prompt")
    outs_p, lses_p, kv_prompt = [], [], []
    for g, w in enumerate(ATTN_WINDOWS):
        o, lse = _attn_prompt(qkv_p[g], band_bias, g)
        outs_p.append(o)
        lses_p.append(lse)
        keep = min(w, seq)
        kv = _kv_tail(xp, attn_gain, wqkv, g, keep).reshape(2, 1, 1, HEADS, HEAD_DIM, keep)
        kv_prompt.append((from_pos_minor(kv[0]), from_pos_minor(kv[1])))
    xp = _merge(xp, outs_p, lses_p, wout, ATTN_DILATIONS, "merge_prompt")
    (yp,), _ = ffn(xp, 2, 1, final_norm, "final_norm", "ffn2_l1_prompt")

    xs = _merge(xs, [d[0] for d in decoded], [d[1] for d in decoded], wout, no_dilation, "merge_step")
    (ys,), _ = ffn(xs, 2, 1, final_norm, "final_norm", "ffn2_l1_step")
    rolled = [from_pos_minor(c) for d in decoded for c in d[2:]]

    return (yp.reshape(x_prompt.shape), ys.reshape(x_sample.shape), pool_state_p, pool_state_s,
            kv_prompt[0][0], kv_prompt[0][1], rolled[0], rolled[1],
            kv_prompt[1][0], kv_prompt[1][1], rolled[2], rolled[3],
            kv_prompt[2][0], kv_prompt[2][1], rolled[4], rolled[5])
```

```python
import functools
import math
from typing import NamedTuple

import numpy as np
import jax
import jax.numpy as jnp
from jax import lax
from jax.experimental import pallas as pl
from jax.experimental.pallas import tpu as pltpu

F32 = jnp.float32
BF16 = jnp.bfloat16

D_MODEL = 1024
D_FF = 2816
RMS_EPS = 1e-6
POOL_WINDOWS = (2, 4, 8, 16)
POOL_GROUP_DIM = D_MODEL // len(POOL_WINDOWS)
POOL_STATE_LEN = max(POOL_WINDOWS) - 1
ATTN_WINDOWS = (128, 512, 2048)
ATTN_DILATIONS = (1, 4, 16)
N_GROUPS = 3
HEAD_DIM = 64
HEADS = 8
GROUP_WIDTH = HEADS * HEAD_DIM
QKV_WIDTH = N_GROUPS * 3 * GROUP_WIDTH
Q_BLOCK = 128
N_KEYS = Q_BLOCK + 1
ATTN_SCALE = HEAD_DIM ** -0.5
N_BUCKETS = 32
MAX_EXACT = N_BUCKETS // 2
MAX_DISTANCE = 2048
NEG_INF = -1e30

V7X_LANES = 128
V7X_VMEM_BYTES = 64 * 1024 * 1024
V7X_VMEM_RESERVE = 6 * 1024 * 1024
ROW_TILE = 512
FFN_ROW_TILE = 1024
FFN_MERGE_ROW_TILE = 512
FFN_SUB_ROWS = 256


def _vmem_limit(estimate_bytes):
    return int(min(V7X_VMEM_BYTES - V7X_VMEM_RESERVE, estimate_bytes))


def _params(n_axes, vmem_bytes):
    return pltpu.CompilerParams(
        dimension_semantics=("arbitrary",) * n_axes,
        vmem_limit_bytes=_vmem_limit(vmem_bytes))


def _resident(shape):
    return pl.BlockSpec(shape, lambda *_: (0,) * len(shape), pipeline_mode=pl.Buffered(1))


def _rms(x, g):
    return x * lax.rsqrt(jnp.mean(x * x, axis=-1, keepdims=True) + RMS_EPS) * g


class _FfnPlan(NamedTuple):
    pre: str
    post: str
    riders: int
    merge_dilations: tuple
    emit_dilations: tuple

    @property
    def n_pre(self):
        return 2 * N_GROUPS + 1 if self.pre == "merge" else 0

    @property
    def n_out(self):
        return {"none": 1, "mixer_input": 2, "final_norm": 1, "attn_input": 1 + len(self.emit_dilations)}[self.post]


def _emit_regrouped(h, out_refs, dilations, slab_ref, j, sub):
    n_slabs = D_MODEL // V7X_LANES
    lanes = lambda c: slice(c * V7X_LANES, (c + 1) * V7X_LANES)
    classes = [(0, h)]
    prev = 1
    for level, (o_ref, d) in enumerate(zip(out_refs, dilations)):
        ratio = d // prev
        if ratio > 1:
            park = slab_ref.at[level % 2]
            finer = []
            for idx, (res, rows) in enumerate(classes):
                n = rows.shape[0]
                for c in range(n_slabs):
                    park[c, idx * n:(idx + 1) * n, :] = rows[:, lanes(c)]
                for q in range(ratio):
                    pick = [park[c, pl.ds(idx * n + q, n // ratio, stride=ratio), :] for c in range(n_slabs)]
                    finer.append((res + prev * q, jnp.concatenate(pick, axis=1)))
            classes, prev = finer, d
        for res, rows in classes:
            n = rows.shape[0]
            o_ref[res, j * n:(j + 1) * n, :] = rows.astype(BF16)


def _ffn_kernel(x_ref, g_ref, wg_ref, wu_ref, wd_ref, *refs, plan):
    n_post = 0 if plan.post == "none" else 1
    n_rider_in, n_rider_out = 7 * plan.riders, 4 * plan.riders
    pos = 0
    pre_refs, pos = refs[pos:pos + plan.n_pre], pos + plan.n_pre
    pg_ref, pos = (refs[pos] if n_post else None), pos + n_post
    rider_in, pos = refs[pos:pos + n_rider_in], pos + n_rider_in
    out_refs, pos = refs[pos:pos + plan.n_out], pos + plan.n_out
    rider_out, pos = refs[pos:pos + n_rider_out], pos + n_rider_out
    scratch = refs[pos:]
    tm = x_ref.shape[0]

    for r in range(plan.riders):
        _decode_attn_item(*rider_in[7 * r:7 * r + 7], *rider_out[4 * r:4 * r + 4])

    n_sub = max(1, tm // FFN_SUB_ROWS)
    sub = tm // n_sub
    n_slabs = D_MODEL // V7X_LANES
    for j in range(n_sub):
        rows = slice(j * sub, (j + 1) * sub)
        x = x_ref[rows, :]
        if plan.pre == "merge":
            attn = _merged_attention(pre_refs[:2 * N_GROUPS], scratch[0].at[j], j, sub, plan.merge_dilations)
            x = x + jnp.dot(attn.astype(BF16), pre_refs[-1][...], preferred_element_type=F32)
        h = _rms(x, g_ref[...]).astype(BF16)
        gate = jnp.dot(h, wg_ref[...], preferred_element_type=F32)
        up = jnp.dot(h, wu_ref[...], preferred_element_type=F32)
        a = (gate * jax.nn.sigmoid(gate) * up).astype(BF16)
        y = x + 0.5 * jnp.dot(a, wd_ref[...], preferred_element_type=F32)
        if plan.post == "final_norm":
            out_refs[0][rows, :] = _rms(y, pg_ref[...])
            continue
        out_refs[0][rows, :] = y
        if plan.post == "mixer_input":
            out_refs[1][rows, :] = _rms(y, pg_ref[...]).astype(BF16)
        elif plan.post == "attn_input":
            _emit_regrouped(_rms(y, pg_ref[...]), out_refs[1:], plan.emit_dilations, scratch[-1].at[j], j, sub)


def _layer_resident(shape, layer):
    return pl.BlockSpec((None,) + shape, lambda *_: (layer,) + (0,) * len(shape), pipeline_mode=pl.Buffered(1))


def _ffn(x, g, wg, wu, wd, layer, post_g, post, name, riders=(), merge=None, emit_dilations=()):
    rows = x.shape[0]
    tm = min(FFN_MERGE_ROW_TILE if merge else FFN_ROW_TILE, rows)
    rider_in, rider_in_specs, rider_out_shapes, rider_out_specs = [], [], [], []
    rider_bytes = 0
    for group, cache_layer, operands in riders:
        cache_k = operands[3]
        batch, quads = cache_k.shape[1], cache_k.shape[2]
        tm = rows // (batch * quads)
        in_specs, out_specs = _decode_attn_specs(cache_k, group, cache_layer, lambda i: (i // quads, i % quads))
        rider_in += list(operands)
        rider_in_specs += in_specs
        rider_out_shapes += _decode_attn_out_shapes(cache_k)
        rider_out_specs += out_specs
        rider_bytes += 10 * STEP_HEADS * HEAD_DIM * cache_k.shape[-1] * 4
    row_spec = pl.BlockSpec((tm, D_MODEL), lambda i: (i, 0))
    vec_spec = _resident((1, D_MODEL))
    plan = _FfnPlan(pre="merge" if merge else "none", post=post, riders=len(riders),
                    merge_dilations=tuple(merge[3]) if merge else (), emit_dilations=tuple(emit_dilations))
    pre_in, pre_specs, scratch, extra_bytes = [], [], [], 0
    if merge:
        outs_g, lses_g, w_out, dilations = merge
        grp_specs = [pl.BlockSpec((d, tm // d, GROUP_WIDTH), lambda i: (0, i, 0)) for d in dilations]
        pre_in = list(outs_g) + list(lses_g) + [w_out]
        pre_specs = grp_specs * 2 + [_resident((GROUP_WIDTH, D_MODEL))]
        sub = min(tm, FFN_SUB_ROWS)
        scratch.append(pltpu.VMEM((tm // sub, 2 * N_GROUPS, GROUP_WIDTH // V7X_LANES, sub, V7X_LANES), F32))
        extra_bytes += tm * GROUP_WIDTH * 4 * (2 * 6 + 6 + 4)
    out_shape = [jax.ShapeDtypeStruct((rows, D_MODEL), F32)]
    out_specs = [row_spec]
    if post == "mixer_input":
        out_shape.append(jax.ShapeDtypeStruct((rows, D_MODEL), BF16))
        out_specs.append(row_spec)
    elif post == "attn_input":
        out_shape += [jax.ShapeDtypeStruct((d, rows // d, D_MODEL), BF16) for d in emit_dilations]
        out_specs += [pl.BlockSpec((d, tm // d, D_MODEL), lambda i: (0, i, 0)) for d in emit_dilations]
        sub = min(tm, FFN_SUB_ROWS)
        scratch.append(pltpu.VMEM((tm // sub, 2, D_MODEL // V7X_LANES, sub, V7X_LANES), F32))
        extra_bytes += tm * D_MODEL * (4 + 2 * 2 * len(emit_dilations))
    sub = min(tm, FFN_SUB_ROWS)
    weights = 3 * D_MODEL * D_FF * 2
    tiles = tm * D_MODEL * (4 * 2 + 4 * 2 + 2 * 2)
    temps = 2 * (sub * D_FF * (4 + 4 + 4 + 2) + sub * D_MODEL * 12)
    post_in = () if post == "none" else (post_g,)
    outs = pl.pallas_call(
        functools.partial(_ffn_kernel, plan=plan),
        out_shape=out_shape + rider_out_shapes,
        grid=(rows // tm,),
        in_specs=[row_spec, _layer_resident((1, D_MODEL), layer), _layer_resident((D_MODEL, D_FF), layer),
                  _layer_resident((D_MODEL, D_FF), layer), _layer_resident((D_FF, D_MODEL), layer)]
        + pre_specs + [vec_spec] * len(post_in) + rider_in_specs,
        out_specs=out_specs + rider_out_specs,
        scratch_shapes=scratch,
        compiler_params=_params(1, weights + tiles + temps + rider_bytes + extra_bytes),
        name=name,
    )(x, g, wg, wu, wd, *pre_in, *post_in, *rider_in)
    n_own = len(out_shape)
    return outs[:n_own], [_decode_attn_results(*outs[n_own + 4 * r:n_own + 4 * r + 4]) for r in range(len(riders))]


def _pool_tail(x, u, sums, counts, wgrp_ref, scale_ref, wout_ref):
    y = x
    for g, w in enumerate(POOL_WINDOWS):
        cols = slice(g * POOL_GROUP_DIM, (g + 1) * POOL_GROUP_DIM)
        z = (sums[g] / counts[g] - u[:, cols]).astype(BF16)
        z = jnp.dot(z, wgrp_ref[g], preferred_element_type=F32) * scale_ref[:, cols]
        y = y + jnp.dot(z.astype(BF16), wout_ref[cols, :], preferred_element_type=F32)
    return y


def _pool_prompt_kernel(x_ref, h_ref, win_ref, wgrp_ref, scale_ref, wout_ref,
                        xo_ref, tail_ref, hist_ref, *, tm):
    i = pl.program_id(0)
    hist = POOL_STATE_LEN + 1

    @pl.when(i == 0)
    def _():
        hist_ref[...] = jnp.zeros((hist, D_MODEL), F32)

    u = jnp.dot(h_ref[...], win_ref[...], preferred_element_type=F32)
    pos = i * tm + lax.broadcasted_iota(jnp.int32, (tm, 1), 0)
    sums, counts = [], []
    for g, w in enumerate(POOL_WINDOWS):
        cols = slice(g * POOL_GROUP_DIM, (g + 1) * POOL_GROUP_DIM)
        s = jnp.concatenate([hist_ref[:, cols], u[:, cols]], axis=0)
        k = 1
        while k < w:
            s = s + pltpu.roll(s, k, 0)
            k *= 2
        sums.append(s[hist:, :])
        counts.append(jnp.minimum(w, pos + 1).astype(F32))
    xo_ref[...] = _pool_tail(x_ref[...], u, sums, counts, wgrp_ref, scale_ref, wout_ref)
    last = u[tm - hist:, :]
    tail_ref[...] = last
    hist_ref[...] = last


def _pool_prompt(x, h, win, wgrp, scale, wout):
    rows = x.shape[0]
    tm = ROW_TILE
    hist = POOL_STATE_LEN + 1
    row_spec = pl.BlockSpec((tm, D_MODEL), lambda i: (i, 0))
    weights = (2 * D_MODEL * D_MODEL + 4 * POOL_GROUP_DIM * POOL_GROUP_DIM) * 2
    tiles = tm * D_MODEL * (4 * 2 + 2 * 2 + 4 * 2) + (tm + hist) * D_MODEL * 4
    temps = tm * D_MODEL * 4 * 6
    return pl.pallas_call(
        functools.partial(_pool_prompt_kernel, tm=tm),
        out_shape=[jax.ShapeDtypeStruct((rows, D_MODEL), F32),
                   jax.ShapeDtypeStruct((hist, D_MODEL), F32)],
        grid=(rows // tm,),
        in_specs=[row_spec, row_spec, _resident((D_MODEL, D_MODEL)),
                  _resident((4, POOL_GROUP_DIM, POOL_GROUP_DIM)), _resident((1, D_MODEL)),
                  _resident((D_MODEL, D_MODEL))],
        out_specs=[row_spec, pl.BlockSpec((hist, D_MODEL), lambda i: (0, 0))],
        scratch_shapes=[pltpu.VMEM((hist, D_MODEL), F32)],
        compiler_params=_params(1, weights + tiles + temps),
        name="pool_prompt",
    )(x, h, win, wgrp, scale, wout)


def _pool_step_kernel(x_ref, h_ref, st_ref, win_ref, wgrp_ref, scale_ref, wout_ref,
                      xo_ref, sto_ref, *, past_len):
    u = jnp.dot(h_ref[...], win_ref[...], preferred_element_type=F32)
    sums, counts = [], []
    for g, w in enumerate(POOL_WINDOWS):
        cols = slice(g * POOL_GROUP_DIM, (g + 1) * POOL_GROUP_DIM)
        acc = u[:, cols]
        for j in range(1, w):
            acc = acc + st_ref[POOL_STATE_LEN - j, :, cols]
        sums.append(acc)
        counts.append(float(min(w, past_len + 1)))
    xo_ref[...] = _pool_tail(x_ref[...], u, sums, counts, wgrp_ref, scale_ref, wout_ref)
    for k in range(POOL_STATE_LEN - 1):
        sto_ref[k] = st_ref[k + 1]
    sto_ref[POOL_STATE_LEN - 1] = u


def _pool_step(x, h, state, win, wgrp, scale, wout, past_len):
    rows = x.shape[0]
    full = lambda shape: pl.BlockSpec(shape, lambda i: (0,) * len(shape))
    return pl.pallas_call(
        functools.partial(_pool_step_kernel, past_len=past_len),
        out_shape=[jax.ShapeDtypeStruct((rows, D_MODEL), F32),
                   jax.ShapeDtypeStruct(state.shape, F32)],
        grid=(1,),
        in_specs=[full((rows, D_MODEL)), full((rows, D_MODEL)), full(state.shape),
                  full((D_MODEL, D_MODEL)), full((4, POOL_GROUP_DIM, POOL_GROUP_DIM)),
                  full((1, D_MODEL)), full((D_MODEL, D_MODEL))],
        out_specs=[full((rows, D_MODEL)), full(state.shape)],
        compiler_params=_params(1, 32 * 1024 * 1024),
        name="pool_step",
    )(x, h, state, win, wgrp, scale, wout)


def _qkv_kernel(*refs, q_scale):
    hs, w_ref, outs = refs[:N_GROUPS], refs[N_GROUPS], refs[N_GROUPS + 1:]
    width = 3 * GROUP_WIDTH
    for g, (h_ref, o_ref) in enumerate(zip(hs, outs)):
        d, sub, _ = h_ref.shape
        h = h_ref[...].reshape(d * sub, D_MODEL)
        res = jnp.dot(h, w_ref[:, g * width:(g + 1) * width], preferred_element_type=F32)
        for r in range(d):
            rows = slice(r * sub, (r + 1) * sub)
            o_ref[r, :, 0:GROUP_WIDTH] = (res[rows, 0:GROUP_WIDTH] * q_scale).astype(o_ref.dtype)
            o_ref[r, :, GROUP_WIDTH:] = res[rows, GROUP_WIDTH:].astype(o_ref.dtype)


def _qkv(hs, w, q_scale, dtype, name):
    rows = hs[0].shape[0] * hs[0].shape[1]
    tm = min(ROW_TILE, rows)
    width = 3 * GROUP_WIDTH
    weights = D_MODEL * QKV_WIDTH * 2
    tiles = N_GROUPS * tm * D_MODEL * 2 * 2 + tm * QKV_WIDTH * jnp.dtype(dtype).itemsize * 2
    temps = tm * width * 4 * 2
    return pl.pallas_call(
        functools.partial(_qkv_kernel, q_scale=q_scale),
        out_shape=[jax.ShapeDtypeStruct(h.shape[:2] + (width,), dtype) for h in hs],
        grid=(rows // tm,),
        in_specs=[pl.BlockSpec((h.shape[0], tm // h.shape[0], D_MODEL), lambda i: (0, i, 0)) for h in hs]
        + [_resident((D_MODEL, QKV_WIDTH))],
        out_specs=[pl.BlockSpec((h.shape[0], tm // h.shape[0], width), lambda i: (0, i, 0)) for h in hs],
        compiler_params=_params(1, weights + tiles + temps),
        name=name,
    )(*hs, w)


def _kv_tail_kernel(h_ref, wk_ref, wv_ref, o_ref):
    h = h_ref[...]
    o_ref[0] = jnp.dot(h, wk_ref[...], preferred_element_type=F32).T
    o_ref[1] = jnp.dot(h, wv_ref[...], preferred_element_type=F32).T


def _kv_tail(h, w, g, window):
    rows = h.shape[0]
    tm = min(window, 256)
    first = (rows - window) // tm
    col = 3 * g
    return pl.pallas_call(
        _kv_tail_kernel,
        out_shape=jax.ShapeDtypeStruct((2, GROUP_WIDTH, window), F32),
        grid=(window // tm,),
        in_specs=[pl.BlockSpec((tm, D_MODEL), lambda i: (first + i, 0)),
                  pl.BlockSpec((D_MODEL, GROUP_WIDTH), lambda i: (0, col + 1)),
                  pl.BlockSpec((D_MODEL, GROUP_WIDTH), lambda i: (0, col + 2))],
        out_specs=pl.BlockSpec((2, GROUP_WIDTH, tm), lambda i: (0, 0, i)),
        compiler_params=_params(1, 16 * 1024 * 1024),
        name="kv_tail_g%d" % g,
    )(h, w, w)


def _bucket_table():
    out = np.zeros((N_GROUPS, N_KEYS), np.int32)
    for g, d in enumerate(ATTN_DILATIONS):
        dist = np.arange(N_KEYS, dtype=np.int32) * d
        distf = np.maximum(dist, 1).astype(np.float32)
        log_b = MAX_EXACT + (np.log(distf / np.float32(MAX_EXACT)) / np.float32(math.log(MAX_DISTANCE / MAX_EXACT))
                             * np.float32(N_BUCKETS - MAX_EXACT)).astype(np.int32)
        log_b = np.minimum(log_b, N_BUCKETS - 1)
        out[g] = np.where(dist < MAX_EXACT, dist, log_b)
    return out


def _band_offsets():
    a = np.arange(Q_BLOCK)[:, None]
    b = np.arange(2 * Q_BLOCK)[None, :]
    j = Q_BLOCK + a - b
    return np.where((j >= 0) & (j <= Q_BLOCK), j, -1).astype(np.int32)


def _bias_kernel(tab_ref, bidx_ref, l0_ref, l1_ref, l2_ref, band_ref, s0_ref, s1_ref, s2_ref, new_ref, *, buckets):
    sub = lax.broadcasted_iota(jnp.int32, (HEADS, V7X_LANES), 0)
    for g, (lidx_ref, step_ref) in enumerate(((l0_ref, s0_ref), (l1_ref, s1_ref), (l2_ref, s2_ref))):
        bidx = bidx_ref[g]
        lidx = lidx_ref[...]
        used = sorted(set(int(v) for v in buckets[g]))
        t = jnp.zeros((HEADS, V7X_LANES), F32)
        for h in range(HEADS):
            t = jnp.where(sub == h, tab_ref[int(buckets[g][0]), g * HEADS + h], t)
        new_ref[g] = t
        for h in range(HEADS):
            tile = jnp.full(bidx.shape, NEG_INF, F32)
            row = jnp.full(lidx.shape, NEG_INF, F32)
            for v in used:
                tile = jnp.where(bidx == v, tab_ref[v, g * HEADS + h], tile)
                row = jnp.where(lidx == v, tab_ref[v, g * HEADS + h], row)
            band_ref[g, h] = tile
            step_ref[h:h + 1, :] = row


def _bias_tables(rel_bias):
    buckets = _bucket_table()
    band = _band_offsets()
    bidx = np.stack([np.where(band >= 0, buckets[g][np.maximum(band, 0)], -1) for g in range(N_GROUPS)])
    lidx = []
    for g, (w, d) in enumerate(zip(ATTN_WINDOWS, ATTN_DILATIONS)):
        pos = np.arange(w)
        lidx.append(np.where(pos % d == 0, buckets[g][(w - pos) // d], -1).astype(np.int32)[None])
    vmem = pl.BlockSpec(memory_space=pltpu.VMEM)
    return pl.pallas_call(
        functools.partial(_bias_kernel, buckets=buckets),
        out_shape=[jax.ShapeDtypeStruct((N_GROUPS, HEADS, Q_BLOCK, 2 * Q_BLOCK), F32)]
        + [jax.ShapeDtypeStruct((HEADS, w), F32) for w in ATTN_WINDOWS]
        + [jax.ShapeDtypeStruct((N_GROUPS, HEADS, V7X_LANES), F32)],
        in_specs=[pl.BlockSpec(memory_space=pltpu.SMEM)] + [vmem] * 4,
        out_specs=[vmem] * 5,
        compiler_params=pltpu.CompilerParams(vmem_limit_bytes=_vmem_limit(24 * 1024 * 1024)),
        name="bias_tables",
    )(rel_bias, jnp.asarray(bidx), *[jnp.asarray(l) for l in lidx])


ATTN_BLOCKS_PER_STEP = 4


def _attn_prompt_kernel(q_ref, kp_ref, kc_ref, vp_ref, vc_ref, bias_ref, o_ref, lse_ref, *, nb):
    low = lax.broadcasted_iota(jnp.int32, (1, V7X_LANES), 1) < HEAD_DIM
    col = lax.broadcasted_iota(jnp.int32, (1, 2 * Q_BLOCK), 1)
    no_prev = jnp.where(jnp.logical_and(pl.program_id(1) == 0, col < Q_BLOCK), NEG_INF, 0.0)
    q = q_ref[...]
    k = jnp.concatenate([kp_ref[...], kc_ref[...]], axis=0)
    v = jnp.concatenate([vp_ref[...], vc_ref[...]], axis=0)
    items = [(b, p, half) for b in range(nb) for p in range(HEADS // 2) for half in range(2)]
    q_rows = lambda b: slice(b * Q_BLOCK, (b + 1) * Q_BLOCK)
    k_rows = lambda b: slice(b * Q_BLOCK, (b + 2) * Q_BLOCK)
    lanes = lambda p: slice(p * V7X_LANES, (p + 1) * V7X_LANES)
    mine = lambda half: low if half == 0 else jnp.logical_not(low)

    scores = {}
    for b, p, half in items:
        qh = jnp.where(mine(half), q[q_rows(b), lanes(p)], jnp.zeros((), BF16))
        s = lax.dot_general(qh, k[k_rows(b), lanes(p)], (((1,), (1,)), ((), ())), preferred_element_type=F32)
        s = s + bias_ref[2 * p + half]
        scores[b, p, half] = s + no_prev if b == 0 else s
    tops = {it: jnp.max(scores[it], axis=-1, keepdims=True) for it in items}
    weights = {it: jnp.exp(scores[it] - tops[it]).astype(BF16) for it in items}
    acc = {}
    for b, p, half in items:
        vh = jnp.where(mine(half), v[k_rows(b), lanes(p)], jnp.ones((), BF16))
        acc[b, p, half] = jnp.dot(weights[b, p, half], vh, preferred_element_type=F32)
    for b in range(nb):
        for p in range(HEADS // 2):
            a0, a1 = acc[b, p, 0], acc[b, p, 1]
            den = pltpu.roll(jnp.where(low, a1, a0), HEAD_DIM, 1)
            o_ref[q_rows(b), lanes(p)] = jnp.where(low, a0, a1) * (1.0 / den)
            lse_ref[q_rows(b), lanes(p)] = jnp.where(low, tops[b, p, 0], tops[b, p, 1]) + jnp.log(den)


def _attn_prompt(qkv, band_bias, g):
    d, rows, _ = qkv.shape
    nb = ATTN_BLOCKS_PER_STEP
    step_rows = nb * Q_BLOCK

    def spec(which, prev):
        if prev:
            return pl.BlockSpec((None, Q_BLOCK, GROUP_WIDTH), lambda r, n: (r, jnp.maximum(nb * n - 1, 0), which))
        return pl.BlockSpec((None, step_rows, GROUP_WIDTH), lambda r, n: (r, n, which))

    out_spec = pl.BlockSpec((None, step_rows, GROUP_WIDTH), lambda r, n: (r, n, 0))
    bias_spec = pl.BlockSpec((None, HEADS, Q_BLOCK, 2 * Q_BLOCK), lambda r, n: (g, 0, 0, 0))
    return pl.pallas_call(
        functools.partial(_attn_prompt_kernel, nb=nb),
        out_shape=[jax.ShapeDtypeStruct((d, rows, GROUP_WIDTH), F32)] * 2,
        grid=(d, rows // step_rows),
        in_specs=[spec(0, False), spec(1, True), spec(1, False), spec(2, True), spec(2, False), bias_spec],
        out_specs=[out_spec, out_spec],
        compiler_params=_params(2, 32 * 1024 * 1024),
        name="attn_prompt_g%d" % g,
    )(qkv, qkv, qkv, qkv, qkv, band_bias)


STEP_HEADS = 4
STEP_ITEM_REFS = 11


def _decode_attn_item(q_ref, kn_ref, vn_ref, kc_ref, vc_ref, bias_ref, bnew_ref,
                      o_ref, lse_ref, ko_ref, vo_ref):
    length = kc_ref.shape[-1]
    last_tile = slice(length - V7X_LANES, length)
    is_last = lax.broadcasted_iota(jnp.int32, (1, V7X_LANES), 1) == V7X_LANES - 1

    def rolled(ref, hh, new_col):
        x = pltpu.roll(ref[hh], length - 1, 1)
        return x[:, :length - V7X_LANES], jnp.where(is_last, new_col, x[:, last_tile])

    for hh in range(STEP_HEADS):
        col = slice(hh, hh + 1)
        q = q_ref[:, col] * ATTN_SCALE
        k_new, v_new = kn_ref[:, col], vn_ref[:, col]
        s = jnp.sum(q * kc_ref[hh], axis=0, keepdims=True) + bias_ref[col, :]
        s_new = jnp.sum(q * k_new, axis=0, keepdims=True) + bnew_ref[col, 0:1]
        m = jnp.maximum(jnp.max(s, axis=-1, keepdims=True), s_new)
        e = jnp.exp(s - m)
        e_new = jnp.exp(s_new - m)
        den = jnp.sum(e, axis=-1, keepdims=True) + e_new
        acc = jnp.sum(e * vc_ref[hh], axis=-1, keepdims=True) + e_new * v_new
        o_ref[:, col] = acc * (1.0 / den)
        lse_ref[:, col] = jnp.broadcast_to(m + jnp.log(den), (HEAD_DIM, 1))
        for cache_ref, out_ref, new_col in ((kc_ref, ko_ref, k_new), (vc_ref, vo_ref, v_new)):
            body, tail = rolled(cache_ref, hh, new_col)
            if length > V7X_LANES:
                out_ref[hh, :, :length - V7X_LANES] = body
            out_ref[hh, :, last_tile] = tail


def _decode_attn_specs(cache_k, g, layer, item_of_step):
    length = cache_k.shape[-1]
    column = pl.BlockSpec((None, None, HEAD_DIM, STEP_HEADS), lambda *idx: item_of_step(*idx) + (0, 0))
    cache_in = pl.BlockSpec((None, None, None, STEP_HEADS, HEAD_DIM, length),
                            lambda *idx: (layer,) + item_of_step(*idx) + (0, 0, 0))
    cache_out = pl.BlockSpec((None, None, None, STEP_HEADS, HEAD_DIM, length),
                             lambda *idx: (0,) + item_of_step(*idx) + (0, 0, 0))
    bias = pl.BlockSpec((None, STEP_HEADS, length), lambda *idx: (item_of_step(*idx)[1], 0, 0))
    bnew = pl.BlockSpec((None, None, STEP_HEADS, V7X_LANES), lambda *idx: (g, item_of_step(*idx)[1], 0, 0))
    return [column, column, column, cache_in, cache_in, bias, bnew], [column, column, cache_out, cache_out]


def _decode_attn_operands(q, k_new, v_new, cache_k, cache_v, bias_row, bias_new):
    batch = q.shape[0]
    quads = HEADS // STEP_HEADS
    cols = lambda a: jnp.swapaxes(a.reshape(batch, quads, STEP_HEADS, HEAD_DIM), 2, 3)
    quad = lambda c: c.reshape(c.shape[:2] + (quads, STEP_HEADS) + c.shape[3:])
    return (cols(q), cols(k_new), cols(v_new), quad(cache_k), quad(cache_v),
            bias_row.reshape(quads, STEP_HEADS, -1), bias_new.reshape(N_GROUPS, quads, STEP_HEADS, V7X_LANES))


def _decode_attn_out_shapes(cache_quads):
    _, batch, quads, _, dim, length = cache_quads.shape
    column = jax.ShapeDtypeStruct((batch, quads, dim, STEP_HEADS), F32)
    cache = jax.ShapeDtypeStruct((1, batch, quads, STEP_HEADS, dim, length), F32)
    return [column, column, cache, cache]


def _decode_attn_results(o, lse, k_roll, v_roll):
    batch = o.shape[0]
    rows = lambda a: jnp.swapaxes(a, 2, 3).reshape(1, batch, GROUP_WIDTH)
    whole = lambda c: c.reshape(c.shape[:2] + (HEADS,) + c.shape[4:])
    return rows(o), rows(lse), whole(k_roll), whole(v_roll)


def _attn_step_kernel(*refs):
    _decode_attn_item(*refs)


def _attn_step(q, k_new, v_new, cache_k, cache_v, bias_row, bias_new, g, layer):
    batch = q.shape[0]
    length = cache_k.shape[-1]
    operands = _decode_attn_operands(q, k_new, v_new, cache_k, cache_v, bias_row, bias_new)
    in_specs, out_specs = _decode_attn_specs(operands[3], g, layer, lambda b, hq: (b, hq))
    cache_block = STEP_HEADS * HEAD_DIM * length * 4
    outs = pl.pallas_call(
        _attn_step_kernel,
        out_shape=_decode_attn_out_shapes(operands[3]),
        grid=(batch, HEADS // STEP_HEADS),
        in_specs=in_specs,
        out_specs=out_specs,
        compiler_params=_params(2, 10 * cache_block + 8 * 1024 * 1024),
        name="attn_step_g%d" % g,
    )(*operands)
    return _decode_attn_results(*outs)


def _merged_attention(refs, slab_ref, j, sub, dilations):
    n_slabs = GROUP_WIDTH // V7X_LANES

    def token_order(ref, d, slot):
        if d == 1:
            return ref[0, j * sub:(j + 1) * sub, :]
        part = slice(j * (sub // d), (j + 1) * (sub // d))
        for r in range(d):
            for c in range(n_slabs):
                slab_ref[slot, c, pl.ds(r, sub // d, stride=d), :] = ref[r, part, c * V7X_LANES:(c + 1) * V7X_LANES]
        return jnp.concatenate([slab_ref[slot, c] for c in range(n_slabs)], axis=1)

    os_ = [token_order(refs[g], d, g) for g, d in enumerate(dilations)]
    ls = [token_order(refs[N_GROUPS + g], d, N_GROUPS + g) for g, d in enumerate(dilations)]
    m = jnp.maximum(jnp.maximum(ls[0], ls[1]), ls[2])
    es = [jnp.exp(l - m) for l in ls]
    inv = 1.0 / (es[0] + es[1] + es[2])
    return es[0] * inv * os_[0] + es[1] * inv * os_[1] + es[2] * inv * os_[2]


def _merge_kernel(x_ref, *refs, tm, dilations):
    w_ref, xo_ref, slab_ref = refs[2 * N_GROUPS:]
    o = _merged_attention(refs[:2 * N_GROUPS], slab_ref, 0, tm, dilations)
    xo_ref[...] = x_ref[...] + jnp.dot(o.astype(BF16), w_ref[...], preferred_element_type=F32)


def _merge(x, outs, lses, w, dilations, name):
    rows = x.shape[0]
    tm = min(ROW_TILE, rows)
    row_spec = pl.BlockSpec((tm, D_MODEL), lambda i: (i, 0))
    grp_specs = [pl.BlockSpec((d, tm // d, GROUP_WIDTH), lambda i: (0, i, 0)) for d in dilations]
    tiles = tm * D_MODEL * 4 * 4 + tm * GROUP_WIDTH * 4 * (2 * 6 + 6) + GROUP_WIDTH * D_MODEL * 2
    temps = tm * GROUP_WIDTH * 4 * 8
    return pl.pallas_call(
        functools.partial(_merge_kernel, tm=tm, dilations=dilations),
        out_shape=jax.ShapeDtypeStruct((rows, D_MODEL), F32),
        grid=(rows // tm,),
        in_specs=[row_spec] + grp_specs * 2 + [_resident((GROUP_WIDTH, D_MODEL))],
        out_specs=row_spec,
        scratch_shapes=[pltpu.VMEM((2 * N_GROUPS, GROUP_WIDTH // V7X_LANES, tm, V7X_LANES), F32)],
        compiler_params=_params(1, tiles + temps),
        name=name,
    )(x, *outs, *lses, w)


def kernel(x_prompt, x_sample, state_pool, cache_k_w128, cache_v_w128, cache_k_w512, cache_v_w512,
           cache_k_w2048, cache_v_w2048, ffn1_norm, ffn1_w_gate, ffn1_w_up, ffn1_w_down, mix_norm,
           pool_w_in, pool_w_group, pool_scale, pool_w_out, attn_w_qkv, attn_w_out, rel_bias,
           ffn2_norm, ffn2_w_gate, ffn2_w_up, ffn2_w_down, final_norm):
    seq = x_prompt.shape[1]
    batch = x_sample.shape[0]
    past_len = cache_k_w2048.shape[2]
    cache_k = (cache_k_w128, cache_k_w512, cache_k_w2048)
    cache_v = (cache_v_w128, cache_v_w512, cache_v_w2048)
    bf = lambda w: w.astype(BF16)
    vec = lambda v: v.reshape(1, D_MODEL)

    xp = x_prompt.reshape(seq, D_MODEL)
    xs = x_sample.reshape(batch, D_MODEL)
    band_bias, *step_bias, new_bias = _bias_tables(rel_bias)
    to_pos_minor = lambda c: jnp.transpose(c, (0, 1, 3, 4, 2))
    from_pos_minor = lambda c: jnp.transpose(c, (0, 1, 4, 2, 3))

    ffn_params = {1: (ffn1_norm[:, None], bf(ffn1_w_gate), bf(ffn1_w_up), bf(ffn1_w_down)),
                  2: (ffn2_norm[:, None], bf(ffn2_w_gate), bf(ffn2_w_up), bf(ffn2_w_down))}

    def ffn(x, which, layer, post_g, post, name, **extra):
        post_g = None if post_g is None else vec(post_g)
        return _ffn(x, *ffn_params[which], layer, post_g, post, name, **extra)

    pool_w = (bf(pool_w_in[0]), bf(pool_w_group[0]), vec(pool_scale[0]), bf(pool_w_out[0]))
    wqkv = bf(attn_w_qkv[0])
    wout = bf(attn_w_out[0])
    attn_gain = vec(mix_norm[1])
    no_dilation = (1,) * N_GROUPS

    (xs, hs), _ = ffn(xs, 1, 0, mix_norm[0], "mixer_input", "ffn1_l0_step")
    xs, pool_state_s = _pool_step(xs, hs, jnp.swapaxes(state_pool[0], 0, 1), *pool_w, past_len)
    pool_state_s = jnp.swapaxes(pool_state_s, 0, 1)[None]
    (xs,), _ = ffn(xs, 2, 0, None, "none", "ffn2_l0_step")
    (xs, hs), _ = ffn(xs, 1, 1, mix_norm[1], "attn_input", "ffn1_l1_step", emit_dilations=(1,))
    qkv_s = _qkv([hs] * N_GROUPS, wqkv, 1.0, F32, "qkv_step")
    decode_jobs = []
    for g in range(N_GROUPS):
        q, k_new, v_new = (qkv_s[g][0, :, t * GROUP_WIDTH:(t + 1) * GROUP_WIDTH].reshape(batch, HEADS, HEAD_DIM)
                           for t in range(3))
        decode_jobs.append((g, 0, _decode_attn_operands(q, k_new, v_new, to_pos_minor(cache_k[g]),
                                                        to_pos_minor(cache_v[g]), step_bias[g], new_bias)))

    (xp, hp), decoded = ffn(xp, 1, 0, mix_norm[0], "mixer_input", "ffn1_l0_prompt", riders=decode_jobs)
    xp, tail = _pool_prompt(xp, hp, *pool_w)
    pool_state_p = tail[1:][None, None]
    (xp,), _ = ffn(xp, 2, 0, None, "none", "ffn2_l0_prompt")

    (xp, *hp), _ = ffn(xp, 1, 1, mix_norm[1], "attn_input", "ffn1_l1_prompt", emit_dilations=ATTN_DILATIONS)
    qkv_p = _qkv(hp, wqkv, ATTN_SCALE, BF16, "qkv_prompt")
    outs_p, lses_p, kv_prompt = [], [], []
    for g, w in enumerate(ATTN_WINDOWS):
        o, lse = _attn_prompt(qkv_p[g], band_bias, g)
        outs_p.append(o)
        lses_p.append(lse)
        keep = min(w, seq)
        kv = _kv_tail(hp[ATTN_DILATIONS.index(1)][0], wqkv, g, keep).reshape(2, 1, 1, HEADS, HEAD_DIM, keep)
        kv_prompt.append((from_pos_minor(kv[0]), from_pos_minor(kv[1])))
    (yp,), _ = ffn(xp, 2, 1, final_norm, "final_norm", "ffn2_l1_prompt",
                   merge=(outs_p, lses_p, wout, ATTN_DILATIONS))

    xs = _merge(xs, [d[0] for d in decoded], [d[1] for d in decoded], wout, no_dilation, "merge_step")
    (ys,), _ = ffn(xs, 2, 1, final_norm, "final_norm", "ffn2_l1_step")
    rolled = [from_pos_minor(c) for d in decoded for c in d[2:]]

    return (yp.reshape(x_prompt.shape), ys.reshape(x_sample.shape), pool_state_p, pool_state_s,
            kv_prompt[0][0], kv_prompt[0][1], rolled[0], rolled[1],
            kv_prompt[1][0], kv_prompt[1][1], rolled[2], rolled[3],
            kv_prompt[2][0], kv_prompt[2][1], rolled[4], rolled[5])
```

```python
import functools
import math
from typing import NamedTuple

import numpy as np
import jax
import jax.numpy as jnp
from jax import lax
from jax.experimental import pallas as pl
from jax.experimental.pallas import tpu as pltpu

F32 = jnp.float32
BF16 = jnp.bfloat16

D_MODEL = 1024
D_FF = 2816
RMS_EPS = 1e-6
POOL_WINDOWS = (2, 4, 8, 16)
POOL_GROUP_DIM = D_MODEL // len(POOL_WINDOWS)
POOL_STATE_LEN = max(POOL_WINDOWS) - 1
ATTN_WINDOWS = (128, 512, 2048)
ATTN_DILATIONS = (1, 4, 16)
N_GROUPS = 3
HEAD_DIM = 64
HEADS = 8
GROUP_WIDTH = HEADS * HEAD_DIM
QKV_WIDTH = N_GROUPS * 3 * GROUP_WIDTH
Q_BLOCK = 128
N_KEYS = Q_BLOCK + 1
ATTN_SCALE = HEAD_DIM ** -0.5
N_BUCKETS = 32
MAX_EXACT = N_BUCKETS // 2
MAX_DISTANCE = 2048
NEG_INF = -1e30

V7X_LANES = 128
V7X_VMEM_BYTES = 64 * 1024 * 1024
V7X_VMEM_RESERVE = 6 * 1024 * 1024
ROW_TILE = 512
FFN_ROW_TILE = 1024
FFN_MERGE_ROW_TILE = 512
FFN_SUB_ROWS = 256


def _vmem_limit(estimate_bytes):
    return int(min(V7X_VMEM_BYTES - V7X_VMEM_RESERVE, estimate_bytes))


def _params(n_axes, vmem_bytes):
    return pltpu.CompilerParams(
        dimension_semantics=("arbitrary",) * n_axes,
        vmem_limit_bytes=_vmem_limit(vmem_bytes))


def _resident(shape):
    return pl.BlockSpec(shape, lambda *_: (0,) * len(shape), pipeline_mode=pl.Buffered(1))


def _rms(x, g):
    return x * lax.rsqrt(jnp.mean(x * x, axis=-1, keepdims=True) + RMS_EPS) * g


class _FfnPlan(NamedTuple):
    pre: str
    post: str
    riders: int
    merge_dilations: tuple
    emit_dilations: tuple

    @property
    def n_pre(self):
        return 2 * N_GROUPS + 1 if self.pre == "merge" else 0

    @property
    def n_out(self):
        return {"none": 1, "mixer_input": 2, "final_norm": 1, "attn_input": 1 + len(self.emit_dilations)}[self.post]


def _emit_regrouped(h, out_refs, dilations, slab_ref, j, sub):
    n_slabs = D_MODEL // V7X_LANES
    lanes = lambda c: slice(c * V7X_LANES, (c + 1) * V7X_LANES)
    classes = [(0, h)]
    prev = 1
    for level, (o_ref, d) in enumerate(zip(out_refs, dilations)):
        ratio = d // prev
        if ratio > 1:
            park = slab_ref.at[level % 2]
            finer = []
            for idx, (res, rows) in enumerate(classes):
                n = rows.shape[0]
                for c in range(n_slabs):
                    park[c, idx * n:(idx + 1) * n, :] = rows[:, lanes(c)]
                for q in range(ratio):
                    pick = [park[c, pl.ds(idx * n + q, n // ratio, stride=ratio), :] for c in range(n_slabs)]
                    finer.append((res + prev * q, jnp.concatenate(pick, axis=1)))
            classes, prev = finer, d
        for res, rows in classes:
            n = rows.shape[0]
            o_ref[res, j * n:(j + 1) * n, :] = rows.astype(BF16)


def _ffn_kernel(x_ref, g_ref, wg_ref, wu_ref, wd_ref, *refs, plan):
    n_post = 0 if plan.post == "none" else 1
    n_rider_in, n_rider_out = (1 + 4 * N_GROUPS) * plan.riders, (1 + 2 * N_GROUPS) * plan.riders
    pos = 0
    pre_refs, pos = refs[pos:pos + plan.n_pre], pos + plan.n_pre
    pg_ref, pos = (refs[pos] if n_post else None), pos + n_post
    rider_in, pos = refs[pos:pos + n_rider_in], pos + n_rider_in
    out_refs, pos = refs[pos:pos + plan.n_out], pos + plan.n_out
    rider_out, pos = refs[pos:pos + n_rider_out], pos + n_rider_out
    scratch = refs[pos:]
    tm = x_ref.shape[0]

    if plan.riders:
        _decode_attn_all(rider_in, rider_out)

    n_sub = max(1, tm // FFN_SUB_ROWS)
    sub = tm // n_sub
    n_slabs = D_MODEL // V7X_LANES
    for j in range(n_sub):
        rows = slice(j * sub, (j + 1) * sub)
        x = x_ref[rows, :]
        if plan.pre == "merge":
            attn = _merged_attention(pre_refs[:2 * N_GROUPS], scratch[0].at[j], j, sub, plan.merge_dilations)
            x = x + jnp.dot(attn.astype(BF16), pre_refs[-1][...], preferred_element_type=F32)
        h = _rms(x, g_ref[...]).astype(BF16)
        gate = jnp.dot(h, wg_ref[...], preferred_element_type=F32)
        up = jnp.dot(h, wu_ref[...], preferred_element_type=F32)
        a = (gate * jax.nn.sigmoid(gate) * up).astype(BF16)
        y = x + 0.5 * jnp.dot(a, wd_ref[...], preferred_element_type=F32)
        if plan.post == "final_norm":
            out_refs[0][rows, :] = _rms(y, pg_ref[...])
            continue
        out_refs[0][rows, :] = y
        if plan.post == "mixer_input":
            out_refs[1][rows, :] = _rms(y, pg_ref[...]).astype(BF16)
        elif plan.post == "attn_input":
            _emit_regrouped(_rms(y, pg_ref[...]), out_refs[1:], plan.emit_dilations, scratch[-1].at[j], j, sub)


def _layer_resident(shape, layer):
    return pl.BlockSpec((None,) + shape, lambda *_: (layer,) + (0,) * len(shape), pipeline_mode=pl.Buffered(1))


def _ffn(x, g, wg, wu, wd, layer, post_g, post, name, decode=None, merge=None, emit_dilations=()):
    rows = x.shape[0]
    tm = min(FFN_MERGE_ROW_TILE if merge else FFN_ROW_TILE, rows)
    rider_in, rider_in_specs, rider_out_shapes, rider_out_specs, rider_bytes = [], [], [], [], 0
    if decode is not None:
        tm = rows // (decode.cols.shape[0] * QUADS)
        rider_in, rider_in_specs, rider_out_shapes, rider_out_specs, rider_bytes = _decode_job_specs(
            decode, lambda i: (i // QUADS, i % QUADS))
    row_spec = pl.BlockSpec((tm, D_MODEL), lambda i: (i, 0))
    vec_spec = _resident((1, D_MODEL))
    plan = _FfnPlan(pre="merge" if merge else "none", post=post, riders=0 if decode is None else 1,
                    merge_dilations=tuple(merge[3]) if merge else (), emit_dilations=tuple(emit_dilations))
    pre_in, pre_specs, scratch, extra_bytes = [], [], [], 0
    if merge:
        outs_g, lses_g, w_out, dilations = merge
        grp_specs = [pl.BlockSpec((d, tm // d, GROUP_WIDTH), lambda i: (0, i, 0)) for d in dilations]
        pre_in = list(outs_g) + list(lses_g) + [w_out]
        pre_specs = grp_specs * 2 + [_resident((GROUP_WIDTH, D_MODEL))]
        sub = min(tm, FFN_SUB_ROWS)
        scratch.append(pltpu.VMEM((tm // sub, 2 * N_GROUPS, GROUP_WIDTH // V7X_LANES, sub, V7X_LANES), F32))
        extra_bytes += tm * GROUP_WIDTH * 4 * (2 * 6 + 6 + 4)
    out_shape = [jax.ShapeDtypeStruct((rows, D_MODEL), F32)]
    out_specs = [row_spec]
    if post == "mixer_input":
        out_shape.append(jax.ShapeDtypeStruct((rows, D_MODEL), BF16))
        out_specs.append(row_spec)
    elif post == "attn_input":
        out_shape += [jax.ShapeDtypeStruct((d, rows // d, D_MODEL), BF16) for d in emit_dilations]
        out_specs += [pl.BlockSpec((d, tm // d, D_MODEL), lambda i: (0, i, 0)) for d in emit_dilations]
        sub = min(tm, FFN_SUB_ROWS)
        scratch.append(pltpu.VMEM((tm // sub, 2, D_MODEL // V7X_LANES, sub, V7X_LANES), F32))
        extra_bytes += tm * D_MODEL * (4 + 2 * 2 * len(emit_dilations))
    sub = min(tm, FFN_SUB_ROWS)
    weights = 3 * D_MODEL * D_FF * 2
    tiles = tm * D_MODEL * (4 * 2 + 4 * 2 + 2 * 2)
    temps = 2 * (sub * D_FF * (4 + 4 + 4 + 2) + sub * D_MODEL * 12)
    post_in = () if post == "none" else (post_g,)
    outs = pl.pallas_call(
        functools.partial(_ffn_kernel, plan=plan),
        out_shape=out_shape + rider_out_shapes,
        grid=(rows // tm,),
        in_specs=[row_spec, _layer_resident((1, D_MODEL), layer), _layer_resident((D_MODEL, D_FF), layer),
                  _layer_resident((D_MODEL, D_FF), layer), _layer_resident((D_FF, D_MODEL), layer)]
        + pre_specs + [vec_spec] * len(post_in) + rider_in_specs,
        out_specs=out_specs + rider_out_specs,
        scratch_shapes=scratch,
        compiler_params=_params(1, weights + tiles + temps + rider_bytes + extra_bytes),
        name=name,
    )(x, g, wg, wu, wd, *pre_in, *post_in, *rider_in)
    n_own = len(out_shape)
    return outs[:n_own], (_decode_job_results(outs[n_own:]) if decode is not None else None)


def _pool_tail(x, u, sums, counts, wgrp_ref, scale_ref, wout_ref):
    y = x
    for g, w in enumerate(POOL_WINDOWS):
        cols = slice(g * POOL_GROUP_DIM, (g + 1) * POOL_GROUP_DIM)
        z = (sums[g] / counts[g] - u[:, cols]).astype(BF16)
        z = jnp.dot(z, wgrp_ref[g], preferred_element_type=F32) * scale_ref[:, cols]
        y = y + jnp.dot(z.astype(BF16), wout_ref[cols, :], preferred_element_type=F32)
    return y


def _pool_prompt_kernel(x_ref, h_ref, win_ref, wgrp_ref, scale_ref, wout_ref,
                        xo_ref, tail_ref, hist_ref, *, tm):
    i = pl.program_id(0)
    hist = POOL_STATE_LEN + 1

    @pl.when(i == 0)
    def _():
        hist_ref[...] = jnp.zeros((hist, D_MODEL), F32)

    u = jnp.dot(h_ref[...], win_ref[...], preferred_element_type=F32)
    pos = i * tm + lax.broadcasted_iota(jnp.int32, (tm, 1), 0)
    sums, counts = [], []
    for g, w in enumerate(POOL_WINDOWS):
        cols = slice(g * POOL_GROUP_DIM, (g + 1) * POOL_GROUP_DIM)
        s = jnp.concatenate([hist_ref[:, cols], u[:, cols]], axis=0)
        k = 1
        while k < w:
            s = s + pltpu.roll(s, k, 0)
            k *= 2
        sums.append(s[hist:, :])
        counts.append(jnp.minimum(w, pos + 1).astype(F32))
    xo_ref[...] = _pool_tail(x_ref[...], u, sums, counts, wgrp_ref, scale_ref, wout_ref)
    last = u[tm - hist:, :]
    tail_ref[...] = last
    hist_ref[...] = last


def _pool_prompt(x, h, win, wgrp, scale, wout):
    rows = x.shape[0]
    tm = ROW_TILE
    hist = POOL_STATE_LEN + 1
    row_spec = pl.BlockSpec((tm, D_MODEL), lambda i: (i, 0))
    weights = (2 * D_MODEL * D_MODEL + 4 * POOL_GROUP_DIM * POOL_GROUP_DIM) * 2
    tiles = tm * D_MODEL * (4 * 2 + 2 * 2 + 4 * 2) + (tm + hist) * D_MODEL * 4
    temps = tm * D_MODEL * 4 * 6
    return pl.pallas_call(
        functools.partial(_pool_prompt_kernel, tm=tm),
        out_shape=[jax.ShapeDtypeStruct((rows, D_MODEL), F32),
                   jax.ShapeDtypeStruct((hist, D_MODEL), F32)],
        grid=(rows // tm,),
        in_specs=[row_spec, row_spec, _resident((D_MODEL, D_MODEL)),
                  _resident((4, POOL_GROUP_DIM, POOL_GROUP_DIM)), _resident((1, D_MODEL)),
                  _resident((D_MODEL, D_MODEL))],
        out_specs=[row_spec, pl.BlockSpec((hist, D_MODEL), lambda i: (0, 0))],
        scratch_shapes=[pltpu.VMEM((hist, D_MODEL), F32)],
        compiler_params=_params(1, weights + tiles + temps),
        name="pool_prompt",
    )(x, h, win, wgrp, scale, wout)


def _pool_step_kernel(x_ref, h_ref, st_ref, win_ref, wgrp_ref, scale_ref, wout_ref,
                      xo_ref, sto_ref, *, past_len):
    u = jnp.dot(h_ref[...], win_ref[...], preferred_element_type=F32)
    sums, counts = [], []
    for g, w in enumerate(POOL_WINDOWS):
        cols = slice(g * POOL_GROUP_DIM, (g + 1) * POOL_GROUP_DIM)
        acc = u[:, cols]
        for j in range(1, w):
            acc = acc + st_ref[POOL_STATE_LEN - j, :, cols]
        sums.append(acc)
        counts.append(float(min(w, past_len + 1)))
    xo_ref[...] = _pool_tail(x_ref[...], u, sums, counts, wgrp_ref, scale_ref, wout_ref)
    for k in range(POOL_STATE_LEN - 1):
        sto_ref[k] = st_ref[k + 1]
    sto_ref[POOL_STATE_LEN - 1] = u


def _pool_step(x, h, state, win, wgrp, scale, wout, past_len):
    rows = x.shape[0]
    full = lambda shape: pl.BlockSpec(shape, lambda i: (0,) * len(shape))
    return pl.pallas_call(
        functools.partial(_pool_step_kernel, past_len=past_len),
        out_shape=[jax.ShapeDtypeStruct((rows, D_MODEL), F32),
                   jax.ShapeDtypeStruct(state.shape, F32)],
        grid=(1,),
        in_specs=[full((rows, D_MODEL)), full((rows, D_MODEL)), full(state.shape),
                  full((D_MODEL, D_MODEL)), full((4, POOL_GROUP_DIM, POOL_GROUP_DIM)),
                  full((1, D_MODEL)), full((D_MODEL, D_MODEL))],
        out_specs=[full((rows, D_MODEL)), full(state.shape)],
        compiler_params=_params(1, 32 * 1024 * 1024),
        name="pool_step",
    )(x, h, state, win, wgrp, scale, wout)


def _qkv_kernel(*refs, q_scale):
    hs, w_ref, outs = refs[:N_GROUPS], refs[N_GROUPS], refs[N_GROUPS + 1:]
    width = 3 * GROUP_WIDTH
    for g, (h_ref, o_ref) in enumerate(zip(hs, outs)):
        d, sub, _ = h_ref.shape
        h = h_ref[...].reshape(d * sub, D_MODEL)
        res = jnp.dot(h, w_ref[:, g * width:(g + 1) * width], preferred_element_type=F32)
        for r in range(d):
            rows = slice(r * sub, (r + 1) * sub)
            o_ref[r, :, 0:GROUP_WIDTH] = (res[rows, 0:GROUP_WIDTH] * q_scale).astype(o_ref.dtype)
            o_ref[r, :, GROUP_WIDTH:] = res[rows, GROUP_WIDTH:].astype(o_ref.dtype)


def _qkv(hs, w, q_scale, dtype, name):
    rows = hs[0].shape[0] * hs[0].shape[1]
    tm = min(ROW_TILE, rows)
    width = 3 * GROUP_WIDTH
    weights = D_MODEL * QKV_WIDTH * 2
    tiles = N_GROUPS * tm * D_MODEL * 2 * 2 + tm * QKV_WIDTH * jnp.dtype(dtype).itemsize * 2
    temps = tm * width * 4 * 2
    return pl.pallas_call(
        functools.partial(_qkv_kernel, q_scale=q_scale),
        out_shape=[jax.ShapeDtypeStruct(h.shape[:2] + (width,), dtype) for h in hs],
        grid=(rows // tm,),
        in_specs=[pl.BlockSpec((h.shape[0], tm // h.shape[0], D_MODEL), lambda i: (0, i, 0)) for h in hs]
        + [_resident((D_MODEL, QKV_WIDTH))],
        out_specs=[pl.BlockSpec((h.shape[0], tm // h.shape[0], width), lambda i: (0, i, 0)) for h in hs],
        compiler_params=_params(1, weights + tiles + temps),
        name=name,
    )(*hs, w)


def _kv_tail_kernel(h_ref, wk_ref, wv_ref, o_ref):
    h = h_ref[...]
    o_ref[0] = jnp.dot(h, wk_ref[...], preferred_element_type=F32).T
    o_ref[1] = jnp.dot(h, wv_ref[...], preferred_element_type=F32).T


def _kv_tail(h, w, g, window):
    rows = h.shape[0]
    tm = min(window, 256)
    first = (rows - window) // tm
    col = 3 * g
    return pl.pallas_call(
        _kv_tail_kernel,
        out_shape=jax.ShapeDtypeStruct((2, GROUP_WIDTH, window), F32),
        grid=(window // tm,),
        in_specs=[pl.BlockSpec((tm, D_MODEL), lambda i: (first + i, 0)),
                  pl.BlockSpec((D_MODEL, GROUP_WIDTH), lambda i: (0, col + 1)),
                  pl.BlockSpec((D_MODEL, GROUP_WIDTH), lambda i: (0, col + 2))],
        out_specs=pl.BlockSpec((2, GROUP_WIDTH, tm), lambda i: (0, 0, i)),
        compiler_params=_params(1, 16 * 1024 * 1024),
        name="kv_tail_g%d" % g,
    )(h, w, w)


def _bucket_table():
    out = np.zeros((N_GROUPS, N_KEYS), np.int32)
    for g, d in enumerate(ATTN_DILATIONS):
        dist = np.arange(N_KEYS, dtype=np.int32) * d
        distf = np.maximum(dist, 1).astype(np.float32)
        log_b = MAX_EXACT + (np.log(distf / np.float32(MAX_EXACT)) / np.float32(math.log(MAX_DISTANCE / MAX_EXACT))
                             * np.float32(N_BUCKETS - MAX_EXACT)).astype(np.int32)
        log_b = np.minimum(log_b, N_BUCKETS - 1)
        out[g] = np.where(dist < MAX_EXACT, dist, log_b)
    return out


def _band_offsets():
    a = np.arange(Q_BLOCK)[:, None]
    b = np.arange(2 * Q_BLOCK)[None, :]
    j = Q_BLOCK + a - b
    return np.where((j >= 0) & (j <= Q_BLOCK), j, -1).astype(np.int32)


def _bias_kernel(tab_ref, bidx_ref, l0_ref, l1_ref, l2_ref, band_ref, s0_ref, s1_ref, s2_ref, new_ref, *, buckets):
    sub = lax.broadcasted_iota(jnp.int32, (HEADS, V7X_LANES), 0)
    for g, (lidx_ref, step_ref) in enumerate(((l0_ref, s0_ref), (l1_ref, s1_ref), (l2_ref, s2_ref))):
        bidx = bidx_ref[g]
        lidx = lidx_ref[...]
        used = sorted(set(int(v) for v in buckets[g]))
        t = jnp.zeros((HEADS, V7X_LANES), F32)
        for h in range(HEADS):
            t = jnp.where(sub == h, tab_ref[int(buckets[g][0]), g * HEADS + h], t)
        new_ref[g] = t
        for h in range(HEADS):
            tile = jnp.full(bidx.shape, NEG_INF, F32)
            row = jnp.full(lidx.shape, NEG_INF, F32)
            for v in used:
                tile = jnp.where(bidx == v, tab_ref[v, g * HEADS + h], tile)
                row = jnp.where(lidx == v, tab_ref[v, g * HEADS + h], row)
            band_ref[g, h] = tile
            step_ref[h:h + 1, :] = row


def _bias_tables(rel_bias):
    buckets = _bucket_table()
    band = _band_offsets()
    bidx = np.stack([np.where(band >= 0, buckets[g][np.maximum(band, 0)], -1) for g in range(N_GROUPS)])
    lidx = []
    for g, (w, d) in enumerate(zip(ATTN_WINDOWS, ATTN_DILATIONS)):
        pos = np.arange(w)
        lidx.append(np.where(pos % d == 0, buckets[g][(w - pos) // d], -1).astype(np.int32)[None])
    vmem = pl.BlockSpec(memory_space=pltpu.VMEM)
    return pl.pallas_call(
        functools.partial(_bias_kernel, buckets=buckets),
        out_shape=[jax.ShapeDtypeStruct((N_GROUPS, HEADS, Q_BLOCK, 2 * Q_BLOCK), F32)]
        + [jax.ShapeDtypeStruct((HEADS, w), F32) for w in ATTN_WINDOWS]
        + [jax.ShapeDtypeStruct((N_GROUPS, HEADS, V7X_LANES), F32)],
        in_specs=[pl.BlockSpec(memory_space=pltpu.SMEM)] + [vmem] * 4,
        out_specs=[vmem] * 5,
        compiler_params=pltpu.CompilerParams(vmem_limit_bytes=_vmem_limit(24 * 1024 * 1024)),
        name="bias_tables",
    )(rel_bias, jnp.asarray(bidx), *[jnp.asarray(l) for l in lidx])


ATTN_BLOCKS_PER_STEP = 4


def _attn_prompt_kernel(q_ref, kp_ref, kc_ref, vp_ref, vc_ref, bias_ref, o_ref, lse_ref, *, nb):
    low = lax.broadcasted_iota(jnp.int32, (1, V7X_LANES), 1) < HEAD_DIM
    col = lax.broadcasted_iota(jnp.int32, (1, 2 * Q_BLOCK), 1)
    no_prev = jnp.where(jnp.logical_and(pl.program_id(1) == 0, col < Q_BLOCK), NEG_INF, 0.0)
    q = q_ref[...]
    k = jnp.concatenate([kp_ref[...], kc_ref[...]], axis=0)
    v = jnp.concatenate([vp_ref[...], vc_ref[...]], axis=0)
    items = [(b, p, half) for b in range(nb) for p in range(HEADS // 2) for half in range(2)]
    q_rows = lambda b: slice(b * Q_BLOCK, (b + 1) * Q_BLOCK)
    k_rows = lambda b: slice(b * Q_BLOCK, (b + 2) * Q_BLOCK)
    lanes = lambda p: slice(p * V7X_LANES, (p + 1) * V7X_LANES)
    mine = lambda half: low if half == 0 else jnp.logical_not(low)

    scores = {}
    for b, p, half in items:
        qh = jnp.where(mine(half), q[q_rows(b), lanes(p)], jnp.zeros((), BF16))
        s = lax.dot_general(qh, k[k_rows(b), lanes(p)], (((1,), (1,)), ((), ())), preferred_element_type=F32)
        s = s + bias_ref[2 * p + half]
        scores[b, p, half] = s + no_prev if b == 0 else s
    tops = {it: jnp.max(scores[it], axis=-1, keepdims=True) for it in items}
    weights = {it: jnp.exp(scores[it] - tops[it]).astype(BF16) for it in items}
    acc = {}
    for b, p, half in items:
        vh = jnp.where(mine(half), v[k_rows(b), lanes(p)], jnp.ones((), BF16))
        acc[b, p, half] = jnp.dot(weights[b, p, half], vh, preferred_element_type=F32)
    for b in range(nb):
        for p in range(HEADS // 2):
            a0, a1 = acc[b, p, 0], acc[b, p, 1]
            den = pltpu.roll(jnp.where(low, a1, a0), HEAD_DIM, 1)
            o_ref[q_rows(b), lanes(p)] = jnp.where(low, a0, a1) * (1.0 / den)
            lse_ref[q_rows(b), lanes(p)] = jnp.where(low, tops[b, p, 0], tops[b, p, 1]) + jnp.log(den)


def _attn_prompt(qkv, band_bias, g):
    d, rows, _ = qkv.shape
    nb = ATTN_BLOCKS_PER_STEP
    step_rows = nb * Q_BLOCK

    def spec(which, prev):
        if prev:
            return pl.BlockSpec((None, Q_BLOCK, GROUP_WIDTH), lambda r, n: (r, jnp.maximum(nb * n - 1, 0), which))
        return pl.BlockSpec((None, step_rows, GROUP_WIDTH), lambda r, n: (r, n, which))

    out_spec = pl.BlockSpec((None, step_rows, GROUP_WIDTH), lambda r, n: (r, n, 0))
    bias_spec = pl.BlockSpec((None, HEADS, Q_BLOCK, 2 * Q_BLOCK), lambda r, n: (g, 0, 0, 0))
    return pl.pallas_call(
        functools.partial(_attn_prompt_kernel, nb=nb),
        out_shape=[jax.ShapeDtypeStruct((d, rows, GROUP_WIDTH), F32)] * 2,
        grid=(d, rows // step_rows),
        in_specs=[spec(0, False), spec(1, True), spec(1, False), spec(2, True), spec(2, False), bias_spec],
        out_specs=[out_spec, out_spec],
        compiler_params=_params(2, 32 * 1024 * 1024),
        name="attn_prompt_g%d" % g,
    )(qkv, qkv, qkv, qkv, qkv, band_bias)


STEP_HEADS = 4


def _decode_attn_item(q_cols, kn_cols, vn_cols, kc_ref, vc_ref, bias_ref, bnew_ref, ko_ref, vo_ref):
    length = kc_ref.shape[-1]
    last_tile = slice(length - V7X_LANES, length)
    is_last = lax.broadcasted_iota(jnp.int32, (1, V7X_LANES), 1) == V7X_LANES - 1

    def rolled(ref, hh, new_col):
        x = pltpu.roll(ref[hh], length - 1, 1)
        return x[:, :length - V7X_LANES], jnp.where(is_last, new_col, x[:, last_tile])

    outs, lses = [], []
    for hh in range(STEP_HEADS):
        col = slice(hh, hh + 1)
        q = q_cols[:, col] * ATTN_SCALE
        k_new, v_new = kn_cols[:, col], vn_cols[:, col]
        s = jnp.sum(q * kc_ref[hh], axis=0, keepdims=True) + bias_ref[col, :]
        s_new = jnp.sum(q * k_new, axis=0, keepdims=True) + bnew_ref[col, 0:1]
        m = jnp.maximum(jnp.max(s, axis=-1, keepdims=True), s_new)
        e = jnp.exp(s - m)
        e_new = jnp.exp(s_new - m)
        den = jnp.sum(e, axis=-1, keepdims=True) + e_new
        acc = jnp.sum(e * vc_ref[hh], axis=-1, keepdims=True) + e_new * v_new
        outs.append(acc * (1.0 / den))
        lses.append(jnp.broadcast_to(m + jnp.log(den), (HEAD_DIM, 1)))
        for cache_ref, out_ref, new_col in ((kc_ref, ko_ref, k_new), (vc_ref, vo_ref, v_new)):
            body, tail = rolled(cache_ref, hh, new_col)
            if length > V7X_LANES:
                out_ref[hh, :, :length - V7X_LANES] = body
            out_ref[hh, :, last_tile] = tail
    return outs, lses


QUADS = HEADS // STEP_HEADS
DECODE_IN_LANES = N_GROUPS * 3 * STEP_HEADS
DECODE_OUT_LANES = N_GROUPS * 2 * STEP_HEADS


def _decode_attn_all(in_refs, out_refs):
    cols = in_refs[0][...]
    quad = lambda first: cols[:, first * STEP_HEADS:(first + 1) * STEP_HEADS]
    lane = lax.broadcasted_iota(jnp.int32, (1, DECODE_OUT_LANES), 1)
    packed = jnp.zeros((HEAD_DIM, DECODE_OUT_LANES), F32)
    for g in range(N_GROUPS):
        kc, vc, bias, bnew = in_refs[1 + 4 * g:5 + 4 * g]
        ko, vo = out_refs[1 + 2 * g:3 + 2 * g]
        outs, lses = _decode_attn_item(quad(3 * g), quad(3 * g + 1), quad(3 * g + 2), kc, vc, bias, bnew, ko, vo)
        for hh in range(STEP_HEADS):
            packed = jnp.where(lane == (2 * g) * STEP_HEADS + hh, outs[hh], packed)
            packed = jnp.where(lane == (2 * g + 1) * STEP_HEADS + hh, lses[hh], packed)
    out_refs[0][...] = packed


class _DecodeJob(NamedTuple):
    cols: jax.Array
    caches: tuple
    biases: tuple
    bias_new: jax.Array
    layer: int


def _decode_job(qkv_rows, cache_k, cache_v, step_bias, new_bias, layer):
    batch = qkv_rows.shape[0]
    cols = qkv_rows.reshape(batch, N_GROUPS * 3, QUADS, STEP_HEADS, HEAD_DIM)
    cols = jnp.transpose(cols, (0, 2, 4, 1, 3)).reshape(batch, QUADS, HEAD_DIM, DECODE_IN_LANES)
    quad = lambda c: c.reshape(c.shape[:2] + (QUADS, STEP_HEADS) + c.shape[3:])
    return _DecodeJob(cols, tuple((quad(k), quad(v)) for k, v in zip(cache_k, cache_v)),
                      tuple(b.reshape(QUADS, STEP_HEADS, -1) for b in step_bias),
                      new_bias.reshape(N_GROUPS, QUADS, STEP_HEADS, V7X_LANES), layer)


def _decode_job_specs(job, item_of_step):
    batch = job.cols.shape[0]
    item = lambda *idx: tuple(item_of_step(*idx))
    operands = [job.cols]
    in_specs = [pl.BlockSpec((None, None, HEAD_DIM, DECODE_IN_LANES), lambda *idx: item(*idx) + (0, 0))]
    out_shapes = [jax.ShapeDtypeStruct((batch, QUADS, HEAD_DIM, DECODE_OUT_LANES), F32)]
    out_specs = [pl.BlockSpec((None, None, HEAD_DIM, DECODE_OUT_LANES), lambda *idx: item(*idx) + (0, 0))]
    vmem = 0
    for g, ((kc, vc), bias) in enumerate(zip(job.caches, job.biases)):
        length = kc.shape[-1]
        block = (None, None, None, STEP_HEADS, HEAD_DIM, length)
        cache_in = pl.BlockSpec(block, lambda *idx: (job.layer,) + item(*idx) + (0, 0, 0))
        cache_out = pl.BlockSpec(block, lambda *idx: (0,) + item(*idx) + (0, 0, 0))
        operands += [kc, vc, bias, job.bias_new]
        in_specs += [cache_in, cache_in,
                     pl.BlockSpec((None, STEP_HEADS, length), lambda *idx: (item(*idx)[1], 0, 0)),
                     pl.BlockSpec((None, None, STEP_HEADS, V7X_LANES),
                                  functools.partial(lambda *idx, g: (g, item(*idx)[1], 0, 0), g=g))]
        out_shapes += [jax.ShapeDtypeStruct((1,) + kc.shape[1:], F32)] * 2
        out_specs += [cache_out, cache_out]
        vmem += 10 * STEP_HEADS * HEAD_DIM * length * 4
    return operands, in_specs, out_shapes, out_specs, vmem


def _decode_job_results(outs):
    cols = outs[0]
    batch = cols.shape[0]
    cols = cols.reshape(batch, QUADS, HEAD_DIM, N_GROUPS, 2, STEP_HEADS)
    rows = jnp.transpose(cols, (3, 4, 0, 1, 5, 2)).reshape(N_GROUPS, 2, 1, batch, GROUP_WIDTH)
    whole = lambda c: c.reshape(c.shape[:2] + (HEADS,) + c.shape[4:])
    return ([(rows[g, 0], rows[g, 1]) for g in range(N_GROUPS)],
            [(whole(outs[1 + 2 * g]), whole(outs[2 + 2 * g])) for g in range(N_GROUPS)])


def _merged_attention(refs, slab_ref, j, sub, dilations):
    n_slabs = GROUP_WIDTH // V7X_LANES

    def token_order(ref, d, slot):
        if d == 1:
            return ref[0, j * sub:(j + 1) * sub, :]
        part = slice(j * (sub // d), (j + 1) * (sub // d))
        for r in range(d):
            for c in range(n_slabs):
                slab_ref[slot, c, pl.ds(r, sub // d, stride=d), :] = ref[r, part, c * V7X_LANES:(c + 1) * V7X_LANES]
        return jnp.concatenate([slab_ref[slot, c] for c in range(n_slabs)], axis=1)

    os_ = [token_order(refs[g], d, g) for g, d in enumerate(dilations)]
    ls = [token_order(refs[N_GROUPS + g], d, N_GROUPS + g) for g, d in enumerate(dilations)]
    m = jnp.maximum(jnp.maximum(ls[0], ls[1]), ls[2])
    es = [jnp.exp(l - m) for l in ls]
    inv = 1.0 / (es[0] + es[1] + es[2])
    return es[0] * inv * os_[0] + es[1] * inv * os_[1] + es[2] * inv * os_[2]


def _merge_kernel(x_ref, *refs, tm, dilations):
    w_ref, xo_ref, slab_ref = refs[2 * N_GROUPS:]
    o = _merged_attention(refs[:2 * N_GROUPS], slab_ref, 0, tm, dilations)
    xo_ref[...] = x_ref[...] + jnp.dot(o.astype(BF16), w_ref[...], preferred_element_type=F32)


def _merge(x, outs, lses, w, dilations, name):
    rows = x.shape[0]
    tm = min(ROW_TILE, rows)
    row_spec = pl.BlockSpec((tm, D_MODEL), lambda i: (i, 0))
    grp_specs = [pl.BlockSpec((d, tm // d, GROUP_WIDTH), lambda i: (0, i, 0)) for d in dilations]
    tiles = tm * D_MODEL * 4 * 4 + tm * GROUP_WIDTH * 4 * (2 * 6 + 6) + GROUP_WIDTH * D_MODEL * 2
    temps = tm * GROUP_WIDTH * 4 * 8
    return pl.pallas_call(
        functools.partial(_merge_kernel, tm=tm, dilations=dilations),
        out_shape=jax.ShapeDtypeStruct((rows, D_MODEL), F32),
        grid=(rows // tm,),
        in_specs=[row_spec] + grp_specs * 2 + [_resident((GROUP_WIDTH, D_MODEL))],
        out_specs=row_spec,
        scratch_shapes=[pltpu.VMEM((2 * N_GROUPS, GROUP_WIDTH // V7X_LANES, tm, V7X_LANES), F32)],
        compiler_params=_params(1, tiles + temps),
        name=name,
    )(x, *outs, *lses, w)


def kernel(x_prompt, x_sample, state_pool, cache_k_w128, cache_v_w128, cache_k_w512, cache_v_w512,
           cache_k_w2048, cache_v_w2048, ffn1_norm, ffn1_w_gate, ffn1_w_up, ffn1_w_down, mix_norm,
           pool_w_in, pool_w_group, pool_scale, pool_w_out, attn_w_qkv, attn_w_out, rel_bias,
           ffn2_norm, ffn2_w_gate, ffn2_w_up, ffn2_w_down, final_norm):
    seq = x_prompt.shape[1]
    batch = x_sample.shape[0]
    past_len = cache_k_w2048.shape[2]
    cache_k = (cache_k_w128, cache_k_w512, cache_k_w2048)
    cache_v = (cache_v_w128, cache_v_w512, cache_v_w2048)
    bf = lambda w: w.astype(BF16)
    vec = lambda v: v.reshape(1, D_MODEL)

    xp = x_prompt.reshape(seq, D_MODEL)
    xs = x_sample.reshape(batch, D_MODEL)
    band_bias, *step_bias, new_bias = _bias_tables(rel_bias)
    to_pos_minor = lambda c: jnp.transpose(c, (0, 1, 3, 4, 2))
    from_pos_minor = lambda c: jnp.transpose(c, (0, 1, 4, 2, 3))

    ffn_params = {1: (ffn1_norm[:, None], bf(ffn1_w_gate), bf(ffn1_w_up), bf(ffn1_w_down)),
                  2: (ffn2_norm[:, None], bf(ffn2_w_gate), bf(ffn2_w_up), bf(ffn2_w_down))}

    def ffn(x, which, layer, post_g, post, name, **extra):
        post_g = None if post_g is None else vec(post_g)
        return _ffn(x, *ffn_params[which], layer, post_g, post, name, **extra)

    pool_w = (bf(pool_w_in[0]), bf(pool_w_group[0]), vec(pool_scale[0]), bf(pool_w_out[0]))
    wqkv = bf(attn_w_qkv[0])
    wout = bf(attn_w_out[0])
    no_dilation = (1,) * N_GROUPS

    (xp, hp), _ = ffn(xp, 1, 0, mix_norm[0], "mixer_input", "ffn1_l0_prompt")
    xp, tail = _pool_prompt(xp, hp, *pool_w)
    pool_state_p = tail[1:][None, None]

    (xs, hs), _ = ffn(xs, 1, 0, mix_norm[0], "mixer_input", "ffn1_l0_step")
    xs, pool_state_s = _pool_step(xs, hs, jnp.swapaxes(state_pool[0], 0, 1), *pool_w, past_len)
    pool_state_s = jnp.swapaxes(pool_state_s, 0, 1)[None]
    (xs,), _ = ffn(xs, 2, 0, None, "none", "ffn2_l0_step")
    (xs, hs), _ = ffn(xs, 1, 1, mix_norm[1], "attn_input", "ffn1_l1_step", emit_dilations=(1,))
    qkv_s = jnp.concatenate([part[0] for part in _qkv([hs] * N_GROUPS, wqkv, 1.0, F32, "qkv_step")], axis=1)
    decode = _decode_job(qkv_s, [to_pos_minor(c) for c in cache_k], [to_pos_minor(c) for c in cache_v],
                         step_bias, new_bias, 0)

    (xp,), (decoded, rolled) = ffn(xp, 2, 0, None, "none", "ffn2_l0_prompt", decode=decode)

    (xp, *hp), _ = ffn(xp, 1, 1, mix_norm[1], "attn_input", "ffn1_l1_prompt", emit_dilations=ATTN_DILATIONS)
    qkv_p = _qkv(hp, wqkv, ATTN_SCALE, BF16, "qkv_prompt")
    outs_p, lses_p, kv_prompt = [], [], []
    for g, w in enumerate(ATTN_WINDOWS):
        o, lse = _attn_prompt(qkv_p[g], band_bias, g)
        outs_p.append(o)
        lses_p.append(lse)
        keep = min(w, seq)
        kv = _kv_tail(hp[ATTN_DILATIONS.index(1)][0], wqkv, g, keep).reshape(2, 1, 1, HEADS, HEAD_DIM, keep)
        kv_prompt.append((from_pos_minor(kv[0]), from_pos_minor(kv[1])))
    (yp,), _ = ffn(xp, 2, 1, final_norm, "final_norm", "ffn2_l1_prompt",
                   merge=(outs_p, lses_p, wout, ATTN_DILATIONS))

    xs = _merge(xs, [d[0] for d in decoded], [d[1] for d in decoded], wout, no_dilation, "merge_step")
    (ys,), _ = ffn(xs, 2, 1, final_norm, "final_norm", "ffn2_l1_step")
    rolled = [from_pos_minor(c) for pair in rolled for c in pair]

    return (yp.reshape(x_prompt.shape), ys.reshape(x_sample.shape), pool_state_p, pool_state_s,
            kv_prompt[0][0], kv_prompt[0][1], rolled[0], rolled[1],
            kv_prompt[1][0], kv_prompt[1][1], rolled[2], rolled[3],
            kv_prompt[2][0], kv_prompt[2][1], rolled[4], rolled[5])
```

```python
import functools
import math
from typing import NamedTuple

import numpy as np
import jax
import jax.numpy as jnp
from jax import lax
from jax.experimental import pallas as pl
from jax.experimental.pallas import tpu as pltpu

F32 = jnp.float32
BF16 = jnp.bfloat16

D_MODEL = 1024
D_FF = 2816
RMS_EPS = 1e-6
POOL_WINDOWS = (2, 4, 8, 16)
POOL_GROUP_DIM = D_MODEL // len(POOL_WINDOWS)
POOL_STATE_LEN = max(POOL_WINDOWS) - 1
ATTN_WINDOWS = (128, 512, 2048)
ATTN_DILATIONS = (1, 4, 16)
N_GROUPS = 3
HEAD_DIM = 64
HEADS = 8
GROUP_WIDTH = HEADS * HEAD_DIM
QKV_WIDTH = N_GROUPS * 3 * GROUP_WIDTH
Q_BLOCK = 128
N_KEYS = Q_BLOCK + 1
ATTN_SCALE = HEAD_DIM ** -0.5
LOG2E = math.log2(math.e)
LN2 = math.log(2.0)
N_BUCKETS = 32
MAX_EXACT = N_BUCKETS // 2
MAX_DISTANCE = 2048
NEG_INF = -1e30

V7X_LANES = 128
V7X_VMEM_BYTES = 64 * 1024 * 1024
V7X_VMEM_RESERVE = 6 * 1024 * 1024
ROW_TILE = 512
FFN_ROW_TILE = 1024
FFN_MERGE_ROW_TILE = 512
FFN_SUB_ROWS = 256


def _vmem_limit(estimate_bytes):
    return int(min(V7X_VMEM_BYTES - V7X_VMEM_RESERVE, estimate_bytes))


def _params(n_axes, vmem_bytes, rows=None):
    if rows is not None and rows < FFN_SUB_ROWS:
        vmem_bytes = V7X_VMEM_BYTES
    return pltpu.CompilerParams(
        dimension_semantics=("arbitrary",) * n_axes,
        vmem_limit_bytes=_vmem_limit(vmem_bytes))


def _resident(shape):
    return pl.BlockSpec(shape, lambda *_: (0,) * len(shape), pipeline_mode=pl.Buffered(1))


def _rms(x, g):
    return x * lax.rsqrt(jnp.mean(x * x, axis=-1, keepdims=True) + RMS_EPS) * g


class _FfnPlan(NamedTuple):
    pre: str
    post: str
    riders: int
    merge_dilations: tuple
    emit_dilations: tuple

    @property
    def n_pre(self):
        return 2 * N_GROUPS + 1 if self.pre == "merge" else 0

    @property
    def n_out(self):
        return {"none": 1, "mixer_input": 2, "final_norm": 1, "attn_input": 1 + len(self.emit_dilations)}[self.post]


def _emit_regrouped(h, out_refs, dilations, slab_ref, j, sub):
    n_slabs = D_MODEL // V7X_LANES
    lanes = lambda c: slice(c * V7X_LANES, (c + 1) * V7X_LANES)
    classes = [(0, h)]
    prev = 1
    for level, (o_ref, d) in enumerate(zip(out_refs, dilations)):
        ratio = d // prev
        if ratio > 1:
            park = slab_ref.at[level % 2]
            finer = []
            for idx, (res, rows) in enumerate(classes):
                n = rows.shape[0]
                for c in range(n_slabs):
                    park[c, idx * n:(idx + 1) * n, :] = rows[:, lanes(c)]
                for q in range(ratio):
                    pick = [park[c, pl.ds(idx * n + q, n // ratio, stride=ratio), :] for c in range(n_slabs)]
                    finer.append((res + prev * q, jnp.concatenate(pick, axis=1)))
            classes, prev = finer, d
        for res, rows in classes:
            n = rows.shape[0]
            o_ref[res, j * n:(j + 1) * n, :] = rows.astype(BF16)


def _ffn_kernel(x_ref, g_ref, wg_ref, wu_ref, wd_ref, *refs, plan):
    n_post = 0 if plan.post == "none" else 1
    n_rider_in, n_rider_out = (1 + 4 * N_GROUPS) * plan.riders, (1 + 2 * N_GROUPS) * plan.riders
    pos = 0
    pre_refs, pos = refs[pos:pos + plan.n_pre], pos + plan.n_pre
    pg_ref, pos = (refs[pos] if n_post else None), pos + n_post
    rider_in, pos = refs[pos:pos + n_rider_in], pos + n_rider_in
    out_refs, pos = refs[pos:pos + plan.n_out], pos + plan.n_out
    rider_out, pos = refs[pos:pos + n_rider_out], pos + n_rider_out
    scratch = refs[pos:]
    tm = x_ref.shape[0]

    if plan.riders:
        _decode_attn_all(rider_in, rider_out)

    n_sub = max(1, tm // FFN_SUB_ROWS)
    sub = tm // n_sub
    n_slabs = D_MODEL // V7X_LANES
    for j in range(n_sub):
        rows = slice(j * sub, (j + 1) * sub)
        x = x_ref[rows, :]
        if plan.pre == "merge":
            attn = _merged_attention(pre_refs[:2 * N_GROUPS], scratch[0].at[j], j, sub, plan.merge_dilations)
            x = x + jnp.dot(attn.astype(BF16), pre_refs[-1][...], preferred_element_type=F32)
        h = _rms(x, g_ref[...]).astype(BF16)
        gate = jnp.dot(h, wg_ref[...], preferred_element_type=F32)
        up = jnp.dot(h, wu_ref[...], preferred_element_type=F32)
        a = (gate * jax.nn.sigmoid(gate) * up).astype(BF16)
        y = x + 0.5 * jnp.dot(a, wd_ref[...], preferred_element_type=F32)
        if plan.post == "final_norm":
            out_refs[0][rows, :] = _rms(y, pg_ref[...])
            continue
        out_refs[0][rows, :] = y
        if plan.post == "mixer_input":
            out_refs[1][rows, :] = _rms(y, pg_ref[...]).astype(BF16)
        elif plan.post == "attn_input":
            _emit_regrouped(_rms(y, pg_ref[...]), out_refs[1:], plan.emit_dilations, scratch[-1].at[j], j, sub)


def _layer_resident(shape, layer):
    return pl.BlockSpec((None,) + shape, lambda *_: (layer,) + (0,) * len(shape), pipeline_mode=pl.Buffered(1))


def _ffn(x, g, wg, wu, wd, layer, post_g, post, name, decode=None, merge=None, emit_dilations=()):
    rows = x.shape[0]
    tm = min(FFN_MERGE_ROW_TILE if merge else FFN_ROW_TILE, rows)
    rider_in, rider_in_specs, rider_out_shapes, rider_out_specs, rider_bytes = [], [], [], [], 0
    if decode is not None:
        tm = rows // (decode.cols.shape[0] * QUADS)
        rider_in, rider_in_specs, rider_out_shapes, rider_out_specs, rider_bytes = _decode_job_specs(
            decode, lambda i: (i // QUADS, i % QUADS))
    row_spec = pl.BlockSpec((tm, D_MODEL), lambda i: (i, 0))
    vec_spec = _resident((1, D_MODEL))
    plan = _FfnPlan(pre="merge" if merge else "none", post=post, riders=0 if decode is None else 1,
                    merge_dilations=tuple(merge[3]) if merge else (), emit_dilations=tuple(emit_dilations))
    pre_in, pre_specs, scratch, extra_bytes = [], [], [], 0
    if merge:
        outs_g, lses_g, w_out, dilations = merge
        grp_specs = [pl.BlockSpec((d, tm // d, GROUP_WIDTH), lambda i: (0, i, 0)) for d in dilations]
        pre_in = list(outs_g) + list(lses_g) + [w_out]
        pre_specs = grp_specs * 2 + [_resident((GROUP_WIDTH, D_MODEL))]
        sub = min(tm, FFN_SUB_ROWS)
        scratch.append(pltpu.VMEM((tm // sub, 2 * N_GROUPS, GROUP_WIDTH // V7X_LANES, sub, V7X_LANES), F32))
        extra_bytes += tm * GROUP_WIDTH * 4 * (2 * 6 + 6 + 4)
    out_shape = [jax.ShapeDtypeStruct((rows, D_MODEL), F32)]
    out_specs = [row_spec]
    if post == "mixer_input":
        out_shape.append(jax.ShapeDtypeStruct((rows, D_MODEL), BF16))
        out_specs.append(row_spec)
    elif post == "attn_input":
        out_shape += [jax.ShapeDtypeStruct((d, rows // d, D_MODEL), BF16) for d in emit_dilations]
        out_specs += [pl.BlockSpec((d, tm // d, D_MODEL), lambda i: (0, i, 0)) for d in emit_dilations]
        sub = min(tm, FFN_SUB_ROWS)
        scratch.append(pltpu.VMEM((tm // sub, 2, D_MODEL // V7X_LANES, sub, V7X_LANES), F32))
        extra_bytes += tm * D_MODEL * (4 + 2 * 2 * len(emit_dilations))
    sub = min(tm, FFN_SUB_ROWS)
    weights = 3 * D_MODEL * D_FF * 2
    tiles = tm * D_MODEL * (4 * 2 + 4 * 2 + 2 * 2)
    temps = 2 * (sub * D_FF * (4 + 4 + 4 + 2) + sub * D_MODEL * 12)
    post_in = () if post == "none" else (post_g,)
    outs = pl.pallas_call(
        functools.partial(_ffn_kernel, plan=plan),
        out_shape=out_shape + rider_out_shapes,
        grid=(rows // tm,),
        in_specs=[row_spec, _layer_resident((1, D_MODEL), layer), _layer_resident((D_MODEL, D_FF), layer),
                  _layer_resident((D_MODEL, D_FF), layer), _layer_resident((D_FF, D_MODEL), layer)]
        + pre_specs + [vec_spec] * len(post_in) + rider_in_specs,
        out_specs=out_specs + rider_out_specs,
        scratch_shapes=scratch,
        compiler_params=_params(1, weights + tiles + temps + rider_bytes + extra_bytes, rows),
        name=name,
    )(x, g, wg, wu, wd, *pre_in, *post_in, *rider_in)
    n_own = len(out_shape)
    return outs[:n_own], (_decode_job_results(outs[n_own:]) if decode is not None else None)


def _pool_tail(x, u, sums, counts, wgrp_ref, scale_ref, wout_ref):
    y = x
    for g, w in enumerate(POOL_WINDOWS):
        cols = slice(g * POOL_GROUP_DIM, (g + 1) * POOL_GROUP_DIM)
        z = (sums[g] / counts[g] - u[:, cols]).astype(BF16)
        z = jnp.dot(z, wgrp_ref[g], preferred_element_type=F32) * scale_ref[:, cols]
        y = y + jnp.dot(z.astype(BF16), wout_ref[cols, :], preferred_element_type=F32)
    return y


def _pool_prompt_kernel(x_ref, h_ref, win_ref, wgrp_ref, scale_ref, wout_ref,
                        xo_ref, tail_ref, hist_ref, *, tm):
    i = pl.program_id(0)
    hist = POOL_STATE_LEN + 1

    @pl.when(i == 0)
    def _():
        hist_ref[...] = jnp.zeros((hist, D_MODEL), F32)

    u = jnp.dot(h_ref[...], win_ref[...], preferred_element_type=F32)
    pos = i * tm + lax.broadcasted_iota(jnp.int32, (tm, 1), 0)
    sums, counts = [], []
    for g, w in enumerate(POOL_WINDOWS):
        cols = slice(g * POOL_GROUP_DIM, (g + 1) * POOL_GROUP_DIM)
        s = jnp.concatenate([hist_ref[:, cols], u[:, cols]], axis=0)
        k = 1
        while k < w:
            s = s + pltpu.roll(s, k, 0)
            k *= 2
        sums.append(s[hist:, :])
        counts.append(jnp.minimum(w, pos + 1).astype(F32))
    xo_ref[...] = _pool_tail(x_ref[...], u, sums, counts, wgrp_ref, scale_ref, wout_ref)
    last = u[tm - hist:, :]
    tail_ref[...] = last
    hist_ref[...] = last


def _pool_prompt(x, h, win, wgrp, scale, wout):
    rows = x.shape[0]
    tm = ROW_TILE
    hist = POOL_STATE_LEN + 1
    row_spec = pl.BlockSpec((tm, D_MODEL), lambda i: (i, 0))
    weights = (2 * D_MODEL * D_MODEL + 4 * POOL_GROUP_DIM * POOL_GROUP_DIM) * 2
    tiles = tm * D_MODEL * (4 * 2 + 2 * 2 + 4 * 2) + (tm + hist) * D_MODEL * 4
    temps = tm * D_MODEL * 4 * 6
    return pl.pallas_call(
        functools.partial(_pool_prompt_kernel, tm=tm),
        out_shape=[jax.ShapeDtypeStruct((rows, D_MODEL), F32),
                   jax.ShapeDtypeStruct((hist, D_MODEL), F32)],
        grid=(rows // tm,),
        in_specs=[row_spec, row_spec, _resident((D_MODEL, D_MODEL)),
                  _resident((4, POOL_GROUP_DIM, POOL_GROUP_DIM)), _resident((1, D_MODEL)),
                  _resident((D_MODEL, D_MODEL))],
        out_specs=[row_spec, pl.BlockSpec((hist, D_MODEL), lambda i: (0, 0))],
        scratch_shapes=[pltpu.VMEM((hist, D_MODEL), F32)],
        compiler_params=_params(1, weights + tiles + temps),
        name="pool_prompt",
    )(x, h, win, wgrp, scale, wout)


def _pool_step_kernel(x_ref, h_ref, st_ref, win_ref, wgrp_ref, scale_ref, wout_ref,
                      xo_ref, sto_ref, *, past_len):
    u = jnp.dot(h_ref[...], win_ref[...], preferred_element_type=F32)
    sums, counts = [], []
    for g, w in enumerate(POOL_WINDOWS):
        cols = slice(g * POOL_GROUP_DIM, (g + 1) * POOL_GROUP_DIM)
        acc = u[:, cols]
        for j in range(1, w):
            acc = acc + st_ref[POOL_STATE_LEN - j, :, cols]
        sums.append(acc)
        counts.append(float(min(w, past_len + 1)))
    xo_ref[...] = _pool_tail(x_ref[...], u, sums, counts, wgrp_ref, scale_ref, wout_ref)
    for k in range(POOL_STATE_LEN - 1):
        sto_ref[k] = st_ref[k + 1]
    sto_ref[POOL_STATE_LEN - 1] = u


def _pool_step(x, h, state, win, wgrp, scale, wout, past_len):
    rows = x.shape[0]
    full = lambda shape: pl.BlockSpec(shape, lambda i: (0,) * len(shape))
    return pl.pallas_call(
        functools.partial(_pool_step_kernel, past_len=past_len),
        out_shape=[jax.ShapeDtypeStruct((rows, D_MODEL), F32),
                   jax.ShapeDtypeStruct(state.shape, F32)],
        grid=(1,),
        in_specs=[full((rows, D_MODEL)), full((rows, D_MODEL)), full(state.shape),
                  full((D_MODEL, D_MODEL)), full((4, POOL_GROUP_DIM, POOL_GROUP_DIM)),
                  full((1, D_MODEL)), full((D_MODEL, D_MODEL))],
        out_specs=[full((rows, D_MODEL)), full(state.shape)],
        compiler_params=_params(1, 32 * 1024 * 1024, rows),
        name="pool_step",
    )(x, h, state, win, wgrp, scale, wout)


def _qkv_kernel(*refs, q_scale):
    hs, w_ref, outs = refs[:N_GROUPS], refs[N_GROUPS], refs[N_GROUPS + 1:]
    width = 3 * GROUP_WIDTH
    for g, (h_ref, o_ref) in enumerate(zip(hs, outs)):
        d, sub, _ = h_ref.shape
        h = h_ref[...].reshape(d * sub, D_MODEL)
        res = jnp.dot(h, w_ref[:, g * width:(g + 1) * width], preferred_element_type=F32)
        for r in range(d):
            rows = slice(r * sub, (r + 1) * sub)
            o_ref[r, :, 0:GROUP_WIDTH] = (res[rows, 0:GROUP_WIDTH] * q_scale).astype(o_ref.dtype)
            o_ref[r, :, GROUP_WIDTH:] = res[rows, GROUP_WIDTH:].astype(o_ref.dtype)


def _qkv(hs, w, q_scale, dtype, name):
    rows = hs[0].shape[0] * hs[0].shape[1]
    tm = min(ROW_TILE, rows)
    width = 3 * GROUP_WIDTH
    weights = D_MODEL * QKV_WIDTH * 2
    tiles = N_GROUPS * tm * D_MODEL * 2 * 2 + tm * QKV_WIDTH * jnp.dtype(dtype).itemsize * 2
    temps = tm * width * 4 * 2
    return pl.pallas_call(
        functools.partial(_qkv_kernel, q_scale=q_scale),
        out_shape=[jax.ShapeDtypeStruct(h.shape[:2] + (width,), dtype) for h in hs],
        grid=(rows // tm,),
        in_specs=[pl.BlockSpec((h.shape[0], tm // h.shape[0], D_MODEL), lambda i: (0, i, 0)) for h in hs]
        + [_resident((D_MODEL, QKV_WIDTH))],
        out_specs=[pl.BlockSpec((h.shape[0], tm // h.shape[0], width), lambda i: (0, i, 0)) for h in hs],
        compiler_params=_params(1, weights + tiles + temps, rows),
        name=name,
    )(*hs, w)


def _kv_tail_kernel(h_ref, wk_ref, wv_ref, o_ref):
    h = h_ref[...]
    o_ref[0] = jnp.dot(h, wk_ref[...], preferred_element_type=F32).T
    o_ref[1] = jnp.dot(h, wv_ref[...], preferred_element_type=F32).T


def _kv_tail(h, w, g, window):
    rows = h.shape[0]
    tm = min(window, 256)
    first = (rows - window) // tm
    col = 3 * g
    return pl.pallas_call(
        _kv_tail_kernel,
        out_shape=jax.ShapeDtypeStruct((2, GROUP_WIDTH, window), F32),
        grid=(window // tm,),
        in_specs=[pl.BlockSpec((tm, D_MODEL), lambda i: (first + i, 0)),
                  pl.BlockSpec((D_MODEL, GROUP_WIDTH), lambda i: (0, col + 1)),
                  pl.BlockSpec((D_MODEL, GROUP_WIDTH), lambda i: (0, col + 2))],
        out_specs=pl.BlockSpec((2, GROUP_WIDTH, tm), lambda i: (0, 0, i)),
        compiler_params=_params(1, 16 * 1024 * 1024),
        name="kv_tail_g%d" % g,
    )(h, w, w)


def _bucket_table():
    out = np.zeros((N_GROUPS, N_KEYS), np.int32)
    for g, d in enumerate(ATTN_DILATIONS):
        dist = np.arange(N_KEYS, dtype=np.int32) * d
        distf = np.maximum(dist, 1).astype(np.float32)
        log_b = MAX_EXACT + (np.log(distf / np.float32(MAX_EXACT)) / np.float32(math.log(MAX_DISTANCE / MAX_EXACT))
                             * np.float32(N_BUCKETS - MAX_EXACT)).astype(np.int32)
        log_b = np.minimum(log_b, N_BUCKETS - 1)
        out[g] = np.where(dist < MAX_EXACT, dist, log_b)
    return out


def _band_offsets():
    a = np.arange(Q_BLOCK)[:, None]
    b = np.arange(2 * Q_BLOCK)[None, :]
    j = Q_BLOCK + a - b
    return np.where((j >= 0) & (j <= Q_BLOCK), j, -1).astype(np.int32)


def _bias_kernel(tab_ref, bidx_ref, l0_ref, l1_ref, l2_ref, band_ref, s0_ref, s1_ref, s2_ref, new_ref, *, buckets):
    sub = lax.broadcasted_iota(jnp.int32, (HEADS, V7X_LANES), 0)
    for g, (lidx_ref, step_ref) in enumerate(((l0_ref, s0_ref), (l1_ref, s1_ref), (l2_ref, s2_ref))):
        bidx = bidx_ref[g]
        lidx = lidx_ref[...]
        used = sorted(set(int(v) for v in buckets[g]))
        t = jnp.zeros((HEADS, V7X_LANES), F32)
        for h in range(HEADS):
            t = jnp.where(sub == h, tab_ref[int(buckets[g][0]), g * HEADS + h], t)
        new_ref[g] = t
        for h in range(HEADS):
            tile = jnp.full(bidx.shape, NEG_INF, F32)
            row = jnp.full(lidx.shape, NEG_INF, F32)
            for v in used:
                tile = jnp.where(bidx == v, tab_ref[v, g * HEADS + h], tile)
                row = jnp.where(lidx == v, tab_ref[v, g * HEADS + h], row)
            band_ref[g, h] = tile * LOG2E
            step_ref[h:h + 1, :] = row


def _bias_tables(rel_bias):
    buckets = _bucket_table()
    band = _band_offsets()
    bidx = np.stack([np.where(band >= 0, buckets[g][np.maximum(band, 0)], -1) for g in range(N_GROUPS)])
    lidx = []
    for g, (w, d) in enumerate(zip(ATTN_WINDOWS, ATTN_DILATIONS)):
        pos = np.arange(w)
        lidx.append(np.where(pos % d == 0, buckets[g][(w - pos) // d], -1).astype(np.int32)[None])
    vmem = pl.BlockSpec(memory_space=pltpu.VMEM)
    return pl.pallas_call(
        functools.partial(_bias_kernel, buckets=buckets),
        out_shape=[jax.ShapeDtypeStruct((N_GROUPS, HEADS, Q_BLOCK, 2 * Q_BLOCK), F32)]
        + [jax.ShapeDtypeStruct((HEADS, w), F32) for w in ATTN_WINDOWS]
        + [jax.ShapeDtypeStruct((N_GROUPS, HEADS, V7X_LANES), F32)],
        in_specs=[pl.BlockSpec(memory_space=pltpu.SMEM)] + [vmem] * 4,
        out_specs=[vmem] * 5,
        compiler_params=pltpu.CompilerParams(vmem_limit_bytes=_vmem_limit(24 * 1024 * 1024)),
        name="bias_tables",
    )(rel_bias, jnp.asarray(bidx), *[jnp.asarray(l) for l in lidx])


ATTN_BLOCKS_PER_STEP = 4


def _attn_prompt_kernel(q_ref, kp_ref, kc_ref, vp_ref, vc_ref, bias_ref, o_ref, lse_ref, *, nb):
    low = lax.broadcasted_iota(jnp.int32, (1, V7X_LANES), 1) < HEAD_DIM
    col = lax.broadcasted_iota(jnp.int32, (1, 2 * Q_BLOCK), 1)
    no_prev = jnp.where(jnp.logical_and(pl.program_id(1) == 0, col < Q_BLOCK), NEG_INF, 0.0)
    q = q_ref[...]
    k = jnp.concatenate([kp_ref[...], kc_ref[...]], axis=0)
    v = jnp.concatenate([vp_ref[...], vc_ref[...]], axis=0)
    items = [(b, p, half) for b in range(nb) for p in range(HEADS // 2) for half in range(2)]
    q_rows = lambda b: slice(b * Q_BLOCK, (b + 1) * Q_BLOCK)
    k_rows = lambda b: slice(b * Q_BLOCK, (b + 2) * Q_BLOCK)
    lanes = lambda p: slice(p * V7X_LANES, (p + 1) * V7X_LANES)
    mine = lambda half: low if half == 0 else jnp.logical_not(low)

    scores = {}
    for b, p, half in items:
        qh = jnp.where(mine(half), q[q_rows(b), lanes(p)], jnp.zeros((), BF16))
        s = lax.dot_general(qh, k[k_rows(b), lanes(p)], (((1,), (1,)), ((), ())), preferred_element_type=F32)
        s = s + bias_ref[2 * p + half]
        scores[b, p, half] = s + no_prev if b == 0 else s
    tops = {it: jnp.max(scores[it], axis=-1, keepdims=True) for it in items}
    weights = {it: jnp.exp2(scores[it] - tops[it]).astype(BF16) for it in items}
    v_ones = {(p, half): jnp.where(mine(half), v[:, lanes(p)], jnp.ones((), BF16))
              for p in range(HEADS // 2) for half in range(2)}
    acc = {}
    for b, p, half in items:
        acc[b, p, half] = jnp.dot(weights[b, p, half], v_ones[p, half][k_rows(b), :],
                                  preferred_element_type=F32)
    for b in range(nb):
        for p in range(HEADS // 2):
            a0, a1 = acc[b, p, 0], acc[b, p, 1]
            den = pltpu.roll(jnp.where(low, a1, a0), HEAD_DIM, 1)
            o_ref[q_rows(b), lanes(p)] = jnp.where(low, a0, a1) * (1.0 / den)
            lse_ref[q_rows(b), lanes(p)] = jnp.where(low, tops[b, p, 0], tops[b, p, 1]) * LN2 + jnp.log(den)


def _attn_prompt(qkv, band_bias, g):
    d, rows, _ = qkv.shape
    nb = ATTN_BLOCKS_PER_STEP
    step_rows = nb * Q_BLOCK

    def spec(which, prev):
        if prev:
            return pl.BlockSpec((None, Q_BLOCK, GROUP_WIDTH), lambda r, n: (r, jnp.maximum(nb * n - 1, 0), which))
        return pl.BlockSpec((None, step_rows, GROUP_WIDTH), lambda r, n: (r, n, which))

    out_spec = pl.BlockSpec((None, step_rows, GROUP_WIDTH), lambda r, n: (r, n, 0))
    bias_spec = pl.BlockSpec((None, HEADS, Q_BLOCK, 2 * Q_BLOCK), lambda r, n: (g, 0, 0, 0))
    return pl.pallas_call(
        functools.partial(_attn_prompt_kernel, nb=nb),
        out_shape=[jax.ShapeDtypeStruct((d, rows, GROUP_WIDTH), F32)] * 2,
        grid=(d, rows // step_rows),
        in_specs=[spec(0, False), spec(1, True), spec(1, False), spec(2, True), spec(2, False), bias_spec],
        out_specs=[out_spec, out_spec],
        compiler_params=_params(2, 32 * 1024 * 1024),
        name="attn_prompt_g%d" % g,
    )(qkv, qkv, qkv, qkv, qkv, band_bias)


STEP_HEADS = 4


def _decode_attn_item(q_cols, kn_cols, vn_cols, kc_ref, vc_ref, bias_ref, bnew_ref, ko_ref, vo_ref):
    length = kc_ref.shape[-1]
    last_tile = slice(length - V7X_LANES, length)
    is_last = lax.broadcasted_iota(jnp.int32, (1, V7X_LANES), 1) == V7X_LANES - 1

    def rolled(ref, hh, new_col):
        x = pltpu.roll(ref[hh], length - 1, 1)
        return x[:, :length - V7X_LANES], jnp.where(is_last, new_col, x[:, last_tile])

    outs, lses = [], []
    for hh in range(STEP_HEADS):
        col = slice(hh, hh + 1)
        q = q_cols[:, col] * ATTN_SCALE
        k_new, v_new = kn_cols[:, col], vn_cols[:, col]
        s = jnp.sum(q * kc_ref[hh], axis=0, keepdims=True) + bias_ref[col, :]
        s_new = jnp.sum(q * k_new, axis=0, keepdims=True) + bnew_ref[col, 0:1]
        m = jnp.maximum(jnp.max(s, axis=-1, keepdims=True), s_new)
        e = jnp.exp(s - m)
        e_new = jnp.exp(s_new - m)
        den = jnp.sum(e, axis=-1, keepdims=True) + e_new
        acc = jnp.sum(e * vc_ref[hh], axis=-1, keepdims=True) + e_new * v_new
        outs.append(acc * (1.0 / den))
        lses.append(jnp.broadcast_to(m + jnp.log(den), (HEAD_DIM, 1)))
        for cache_ref, out_ref, new_col in ((kc_ref, ko_ref, k_new), (vc_ref, vo_ref, v_new)):
            body, tail = rolled(cache_ref, hh, new_col)
            if length > V7X_LANES:
                out_ref[hh, :, :length - V7X_LANES] = body
            out_ref[hh, :, last_tile] = tail
    return outs, lses


QUADS = HEADS // STEP_HEADS
DECODE_IN_LANES = N_GROUPS * 3 * STEP_HEADS
DECODE_OUT_LANES = N_GROUPS * 2 * STEP_HEADS


def _decode_attn_all(in_refs, out_refs):
    cols = in_refs[0][...]
    quad = lambda first: cols[:, first * STEP_HEADS:(first + 1) * STEP_HEADS]
    lane = lax.broadcasted_iota(jnp.int32, (1, DECODE_OUT_LANES), 1)
    packed = jnp.zeros((HEAD_DIM, DECODE_OUT_LANES), F32)
    for g in range(N_GROUPS):
        kc, vc, bias, bnew = in_refs[1 + 4 * g:5 + 4 * g]
        ko, vo = out_refs[1 + 2 * g:3 + 2 * g]
        outs, lses = _decode_attn_item(quad(3 * g), quad(3 * g + 1), quad(3 * g + 2), kc, vc, bias, bnew, ko, vo)
        for hh in range(STEP_HEADS):
            packed = jnp.where(lane == (2 * g) * STEP_HEADS + hh, outs[hh], packed)
            packed = jnp.where(lane == (2 * g + 1) * STEP_HEADS + hh, lses[hh], packed)
    out_refs[0][...] = packed


class _DecodeJob(NamedTuple):
    cols: jax.Array
    caches: tuple
    biases: tuple
    bias_new: jax.Array
    layer: int


def _decode_job(qkv_rows, cache_k, cache_v, step_bias, new_bias, layer):
    batch = qkv_rows.shape[0]
    cols = qkv_rows.reshape(batch, N_GROUPS * 3, QUADS, STEP_HEADS, HEAD_DIM)
    cols = jnp.transpose(cols, (0, 2, 4, 1, 3)).reshape(batch, QUADS, HEAD_DIM, DECODE_IN_LANES)
    quad = lambda c: c.reshape(c.shape[:2] + (QUADS, STEP_HEADS) + c.shape[3:])
    return _DecodeJob(cols, tuple((quad(k), quad(v)) for k, v in zip(cache_k, cache_v)),
                      tuple(b.reshape(QUADS, STEP_HEADS, -1) for b in step_bias),
                      new_bias.reshape(N_GROUPS, QUADS, STEP_HEADS, V7X_LANES), layer)


def _decode_job_specs(job, item_of_step):
    batch = job.cols.shape[0]
    item = lambda *idx: tuple(item_of_step(*idx))
    operands = [job.cols]
    in_specs = [pl.BlockSpec((None, None, HEAD_DIM, DECODE_IN_LANES), lambda *idx: item(*idx) + (0, 0))]
    out_shapes = [jax.ShapeDtypeStruct((batch, QUADS, HEAD_DIM, DECODE_OUT_LANES), F32)]
    out_specs = [pl.BlockSpec((None, None, HEAD_DIM, DECODE_OUT_LANES), lambda *idx: item(*idx) + (0, 0))]
    vmem = 0
    for g, ((kc, vc), bias) in enumerate(zip(job.caches, job.biases)):
        length = kc.shape[-1]
        block = (None, None, None, STEP_HEADS, HEAD_DIM, length)
        cache_in = pl.BlockSpec(block, lambda *idx: (job.layer,) + item(*idx) + (0, 0, 0))
        cache_out = pl.BlockSpec(block, lambda *idx: (0,) + item(*idx) + (0, 0, 0))
        operands += [kc, vc, bias, job.bias_new]
        in_specs += [cache_in, cache_in,
                     pl.BlockSpec((None, STEP_HEADS, length), lambda *idx: (item(*idx)[1], 0, 0)),
                     pl.BlockSpec((None, None, STEP_HEADS, V7X_LANES),
                                  functools.partial(lambda *idx, g: (g, item(*idx)[1], 0, 0), g=g))]
        out_shapes += [jax.ShapeDtypeStruct((1,) + kc.shape[1:], F32)] * 2
        out_specs += [cache_out, cache_out]
        vmem += 10 * STEP_HEADS * HEAD_DIM * length * 4
    return operands, in_specs, out_shapes, out_specs, vmem


def _decode_job_results(outs):
    cols = outs[0]
    batch = cols.shape[0]
    cols = cols.reshape(batch, QUADS, HEAD_DIM, N_GROUPS, 2, STEP_HEADS)
    rows = jnp.transpose(cols, (3, 4, 0, 1, 5, 2)).reshape(N_GROUPS, 2, 1, batch, GROUP_WIDTH)
    whole = lambda c: c.reshape(c.shape[:2] + (HEADS,) + c.shape[4:])
    return ([(rows[g, 0], rows[g, 1]) for g in range(N_GROUPS)],
            [(whole(outs[1 + 2 * g]), whole(outs[2 + 2 * g])) for g in range(N_GROUPS)])


def _merged_attention(refs, slab_ref, j, sub, dilations):
    n_slabs = GROUP_WIDTH // V7X_LANES

    def token_order(ref, d, slot):
        if d == 1:
            return ref[0, j * sub:(j + 1) * sub, :]
        part = slice(j * (sub // d), (j + 1) * (sub // d))
        for r in range(d):
            for c in range(n_slabs):
                slab_ref[slot, c, pl.ds(r, sub // d, stride=d), :] = ref[r, part, c * V7X_LANES:(c + 1) * V7X_LANES]
        return jnp.concatenate([slab_ref[slot, c] for c in range(n_slabs)], axis=1)

    os_ = [token_order(refs[g], d, g) for g, d in enumerate(dilations)]
    ls = [token_order(refs[N_GROUPS + g], d, N_GROUPS + g) for g, d in enumerate(dilations)]
    m = jnp.maximum(jnp.maximum(ls[0], ls[1]), ls[2])
    es = [jnp.exp(l - m) for l in ls]
    inv = 1.0 / (es[0] + es[1] + es[2])
    return es[0] * inv * os_[0] + es[1] * inv * os_[1] + es[2] * inv * os_[2]


def _merge_kernel(x_ref, *refs, tm, dilations):
    w_ref, xo_ref, slab_ref = refs[2 * N_GROUPS:]
    o = _merged_attention(refs[:2 * N_GROUPS], slab_ref, 0, tm, dilations)
    xo_ref[...] = x_ref[...] + jnp.dot(o.astype(BF16), w_ref[...], preferred_element_type=F32)


def _merge(x, outs, lses, w, dilations, name):
    rows = x.shape[0]
    tm = min(ROW_TILE, rows)
    row_spec = pl.BlockSpec((tm, D_MODEL), lambda i: (i, 0))
    grp_specs = [pl.BlockSpec((d, tm // d, GROUP_WIDTH), lambda i: (0, i, 0)) for d in dilations]
    tiles = tm * D_MODEL * 4 * 4 + tm * GROUP_WIDTH * 4 * (2 * 6 + 6) + GROUP_WIDTH * D_MODEL * 2
    temps = tm * GROUP_WIDTH * 4 * 8
    return pl.pallas_call(
        functools.partial(_merge_kernel, tm=tm, dilations=dilations),
        out_shape=jax.ShapeDtypeStruct((rows, D_MODEL), F32),
        grid=(rows // tm,),
        in_specs=[row_spec] + grp_specs * 2 + [_resident((GROUP_WIDTH, D_MODEL))],
        out_specs=row_spec,
        scratch_shapes=[pltpu.VMEM((2 * N_GROUPS, GROUP_WIDTH // V7X_LANES, tm, V7X_LANES), F32)],
        compiler_params=_params(1, tiles + temps, rows),
        name=name,
    )(x, *outs, *lses, w)


def kernel(x_prompt, x_sample, state_pool, cache_k_w128, cache_v_w128, cache_k_w512, cache_v_w512,
           cache_k_w2048, cache_v_w2048, ffn1_norm, ffn1_w_gate, ffn1_w_up, ffn1_w_down, mix_norm,
           pool_w_in, pool_w_group, pool_scale, pool_w_out, attn_w_qkv, attn_w_out, rel_bias,
           ffn2_norm, ffn2_w_gate, ffn2_w_up, ffn2_w_down, final_norm):
    seq = x_prompt.shape[1]
    batch = x_sample.shape[0]
    past_len = cache_k_w2048.shape[2]
    cache_k = (cache_k_w128, cache_k_w512, cache_k_w2048)
    cache_v = (cache_v_w128, cache_v_w512, cache_v_w2048)
    bf = lambda w: w.astype(BF16)
    vec = lambda v: v.reshape(1, D_MODEL)

    xp = x_prompt.reshape(seq, D_MODEL)
    xs = x_sample.reshape(batch, D_MODEL)
    band_bias, *step_bias, new_bias = _bias_tables(rel_bias)
    to_pos_minor = lambda c: jnp.transpose(c, (0, 1, 3, 4, 2))
    from_pos_minor = lambda c: jnp.transpose(c, (0, 1, 4, 2, 3))

    ffn_params = {1: (ffn1_norm[:, None], bf(ffn1_w_gate), bf(ffn1_w_up), bf(ffn1_w_down)),
                  2: (ffn2_norm[:, None], bf(ffn2_w_gate), bf(ffn2_w_up), bf(ffn2_w_down))}

    def ffn(x, which, layer, post_g, post, name, **extra):
        post_g = None if post_g is None else vec(post_g)
        return _ffn(x, *ffn_params[which], layer, post_g, post, name, **extra)

    pool_w = (bf(pool_w_in[0]), bf(pool_w_group[0]), vec(pool_scale[0]), bf(pool_w_out[0]))
    wqkv = bf(attn_w_qkv[0])
    wout = bf(attn_w_out[0])
    no_dilation = (1,) * N_GROUPS

    (xp, hp), _ = ffn(xp, 1, 0, mix_norm[0], "mixer_input", "ffn1_l0_prompt")
    xp, tail = _pool_prompt(xp, hp, *pool_w)
    pool_state_p = tail[1:][None, None]

    (xs, hs), _ = ffn(xs, 1, 0, mix_norm[0], "mixer_input", "ffn1_l0_step")
    xs, pool_state_s = _pool_step(xs, hs, jnp.swapaxes(state_pool[0], 0, 1), *pool_w, past_len)
    pool_state_s = jnp.swapaxes(pool_state_s, 0, 1)[None]
    (xs,), _ = ffn(xs, 2, 0, None, "none", "ffn2_l0_step")
    (xs, hs), _ = ffn(xs, 1, 1, mix_norm[1], "attn_input", "ffn1_l1_step", emit_dilations=(1,))
    qkv_s = jnp.concatenate([part[0] for part in _qkv([hs] * N_GROUPS, wqkv, 1.0, F32, "qkv_step")], axis=1)
    decode = _decode_job(qkv_s, [to_pos_minor(c) for c in cache_k], [to_pos_minor(c) for c in cache_v],
                         step_bias, new_bias, 0)

    (xp,), (decoded, rolled) = ffn(xp, 2, 0, None, "none", "ffn2_l0_prompt", decode=decode)

    (xp, *hp), _ = ffn(xp, 1, 1, mix_norm[1], "attn_input", "ffn1_l1_prompt", emit_dilations=ATTN_DILATIONS)
    qkv_p = _qkv(hp, wqkv, ATTN_SCALE * LOG2E, BF16, "qkv_prompt")
    outs_p, lses_p, kv_prompt = [], [], []
    for g, w in enumerate(ATTN_WINDOWS):
        o, lse = _attn_prompt(qkv_p[g], band_bias, g)
        outs_p.append(o)
        lses_p.append(lse)
        keep = min(w, seq)
        kv = _kv_tail(hp[ATTN_DILATIONS.index(1)][0], wqkv, g, keep).reshape(2, 1, 1, HEADS, HEAD_DIM, keep)
        kv_prompt.append((from_pos_minor(kv[0]), from_pos_minor(kv[1])))
    (yp,), _ = ffn(xp, 2, 1, final_norm, "final_norm", "ffn2_l1_prompt",
                   merge=(outs_p, lses_p, wout, ATTN_DILATIONS))

    xs = _merge(xs, [d[0] for d in decoded], [d[1] for d in decoded], wout, no_dilation, "merge_step")
    (ys,), _ = ffn(xs, 2, 1, final_norm, "final_norm", "ffn2_l1_step")
    rolled = [from_pos_minor(c) for pair in rolled for c in pair]

    return (yp.reshape(x_prompt.shape), ys.reshape(x_sample.shape), pool_state_p, pool_state_s,
            kv_prompt[0][0], kv_prompt[0][1], rolled[0], rolled[1],
            kv_prompt[1][0], kv_prompt[1][1], rolled[2], rolled[3],
            kv_prompt[2][0], kv_prompt[2][1], rolled[4], rolled[5])
```

```python
import functools
import math
from typing import NamedTuple

import numpy as np
import jax
import jax.numpy as jnp
from jax import lax
from jax.experimental import pallas as pl
from jax.experimental.pallas import tpu as pltpu

F32 = jnp.float32
BF16 = jnp.bfloat16

D_MODEL = 1024
D_FF = 2816
RMS_EPS = 1e-6
POOL_WINDOWS = (2, 4, 8, 16)
POOL_GROUP_DIM = D_MODEL // len(POOL_WINDOWS)
POOL_STATE_LEN = max(POOL_WINDOWS) - 1
ATTN_WINDOWS = (128, 512, 2048)
ATTN_DILATIONS = (1, 4, 16)
N_GROUPS = 3
HEAD_DIM = 64
HEADS = 8
GROUP_WIDTH = HEADS * HEAD_DIM
QKV_WIDTH = N_GROUPS * 3 * GROUP_WIDTH
Q_BLOCK = 128
N_KEYS = Q_BLOCK + 1
ATTN_SCALE = HEAD_DIM ** -0.5
LOG2E = math.log2(math.e)
LN2 = math.log(2.0)
N_BUCKETS = 32
MAX_EXACT = N_BUCKETS // 2
MAX_DISTANCE = 2048
NEG_INF = -1e30

V7X_LANES = 128
V7X_VMEM_BYTES = 64 * 1024 * 1024
V7X_VMEM_RESERVE = 6 * 1024 * 1024
ROW_TILE = 512
FFN_ROW_TILE = 1024
FFN_MERGE_ROW_TILE = 512
FFN_SUB_ROWS = 256


def _vmem_limit(estimate_bytes):
    return int(min(V7X_VMEM_BYTES - V7X_VMEM_RESERVE, estimate_bytes))


def _params(n_axes, vmem_bytes, rows=None):
    if rows is not None and rows < FFN_SUB_ROWS:
        vmem_bytes = V7X_VMEM_BYTES
    return pltpu.CompilerParams(
        dimension_semantics=("arbitrary",) * n_axes,
        vmem_limit_bytes=_vmem_limit(vmem_bytes))


def _resident(shape):
    return pl.BlockSpec(shape, lambda *_: (0,) * len(shape), pipeline_mode=pl.Buffered(1))


def _rms(x, g):
    return x * lax.rsqrt(jnp.mean(x * x, axis=-1, keepdims=True) + RMS_EPS) * g


class _FfnPlan(NamedTuple):
    pre: str
    post: str
    riders: int
    merge_dilations: tuple
    emit_dilations: tuple

    @property
    def n_pre(self):
        return 2 * N_GROUPS + 1 if self.pre == "merge" else 0

    @property
    def n_out(self):
        return {"none": 1, "mixer_input": 2, "final_norm": 1, "attn_input": 1 + len(self.emit_dilations)}[self.post]


def _emit_regrouped(h, out_refs, dilations, slab_ref, j, sub):
    n_slabs = D_MODEL // V7X_LANES
    lanes = lambda c: slice(c * V7X_LANES, (c + 1) * V7X_LANES)
    classes = [(0, h)]
    prev = 1
    for level, (o_ref, d) in enumerate(zip(out_refs, dilations)):
        ratio = d // prev
        if ratio > 1:
            park = slab_ref.at[level % 2]
            finer = []
            for idx, (res, rows) in enumerate(classes):
                n = rows.shape[0]
                for c in range(n_slabs):
                    park[c, idx * n:(idx + 1) * n, :] = rows[:, lanes(c)]
                for q in range(ratio):
                    pick = [park[c, pl.ds(idx * n + q, n // ratio, stride=ratio), :] for c in range(n_slabs)]
                    finer.append((res + prev * q, jnp.concatenate(pick, axis=1)))
            classes, prev = finer, d
        for res, rows in classes:
            n = rows.shape[0]
            o_ref[res, j * n:(j + 1) * n, :] = rows.astype(BF16)


def _ffn_kernel(x_ref, g_ref, wg_ref, wu_ref, wd_ref, *refs, plan):
    n_post = 0 if plan.post == "none" else 1
    n_rider_in, n_rider_out = (1 + 4 * N_GROUPS) * plan.riders, (1 + 2 * N_GROUPS) * plan.riders
    pos = 0
    pre_refs, pos = refs[pos:pos + plan.n_pre], pos + plan.n_pre
    pg_ref, pos = (refs[pos] if n_post else None), pos + n_post
    rider_in, pos = refs[pos:pos + n_rider_in], pos + n_rider_in
    out_refs, pos = refs[pos:pos + plan.n_out], pos + plan.n_out
    rider_out, pos = refs[pos:pos + n_rider_out], pos + n_rider_out
    scratch = refs[pos:]
    tm = x_ref.shape[0]

    if plan.riders:
        _decode_attn_all(rider_in, rider_out)

    n_sub = max(1, tm // FFN_SUB_ROWS)
    sub = tm // n_sub
    n_slabs = D_MODEL // V7X_LANES
    for j in range(n_sub):
        rows = slice(j * sub, (j + 1) * sub)
        x = x_ref[rows, :]
        if plan.pre == "merge":
            attn = _merged_attention(pre_refs[:2 * N_GROUPS], scratch[0].at[j], j, sub, plan.merge_dilations)
            x = x + jnp.dot(attn.astype(BF16), pre_refs[-1][...], preferred_element_type=F32)
        h = _rms(x, g_ref[...]).astype(BF16)
        gate = jnp.dot(h, wg_ref[...], preferred_element_type=F32)
        up = jnp.dot(h, wu_ref[...], preferred_element_type=F32)
        a = (gate * jax.nn.sigmoid(gate) * up).astype(BF16)
        y = x + 0.5 * jnp.dot(a, wd_ref[...], preferred_element_type=F32)
        if plan.post == "final_norm":
            out_refs[0][rows, :] = _rms(y, pg_ref[...])
            continue
        out_refs[0][rows, :] = y
        if plan.post == "mixer_input":
            out_refs[1][rows, :] = _rms(y, pg_ref[...]).astype(BF16)
        elif plan.post == "attn_input":
            _emit_regrouped(_rms(y, pg_ref[...]), out_refs[1:], plan.emit_dilations, scratch[-1].at[j], j, sub)


FFN_STEP_CHUNK = 256


def _ffn_step_kernel(x_ref, g_ref, wg_ref, wu_ref, wd_ref, *refs, post, n_chunks):
    n_post = 0 if post == "none" else 1
    n_y = 1 if post in ("none", "final_norm") else 2
    pg_ref = refs[0] if n_post else None
    y_refs = refs[n_post:n_post + n_y]
    wgo_ref, wuo_ref, wdo_ref, h_ref, acc_ref = refs[n_post + n_y:]
    c = pl.program_id(0)

    @pl.when(c == 0)
    def _():
        h_ref[...] = _rms(x_ref[...], g_ref[...]).astype(BF16)
        acc_ref[...] = jnp.zeros(acc_ref.shape, F32)

    wg, wu, wd = wg_ref[...].astype(BF16), wu_ref[...].astype(BF16), wd_ref[...].astype(BF16)
    wgo_ref[...], wuo_ref[...], wdo_ref[...] = wg, wu, wd
    h = h_ref[...]
    gate = jnp.dot(h, wg, preferred_element_type=F32)
    up = jnp.dot(h, wu, preferred_element_type=F32)
    a = (gate * jax.nn.sigmoid(gate) * up).astype(BF16)
    acc_ref[...] += jnp.dot(a, wd, preferred_element_type=F32)

    @pl.when(c == n_chunks - 1)
    def _():
        y = x_ref[...] + 0.5 * acc_ref[...]
        if post == "final_norm":
            y_refs[0][...] = _rms(y, pg_ref[...])
        else:
            y_refs[0][...] = y
            if post == "mixer_input":
                y_refs[1][...] = _rms(y, pg_ref[...]).astype(BF16)
            elif post == "attn_input":
                y_refs[1][0] = _rms(y, pg_ref[...]).astype(BF16)


def _ffn_step(x, g, wg, wu, wd, layer, post_g, post, name):
    rows = x.shape[0]
    ck = FFN_STEP_CHUNK
    n_chunks = D_FF // ck
    whole = lambda shape: pl.BlockSpec(shape, lambda c: (0,) * len(shape))
    out_shape = [jax.ShapeDtypeStruct((rows, D_MODEL), F32)]
    out_specs = [whole((rows, D_MODEL))]
    if post == "mixer_input":
        out_shape.append(jax.ShapeDtypeStruct((rows, D_MODEL), BF16))
        out_specs.append(whole((rows, D_MODEL)))
    elif post == "attn_input":
        out_shape.append(jax.ShapeDtypeStruct((1, rows, D_MODEL), BF16))
        out_specs.append(whole((1, rows, D_MODEL)))
    out_shape += [jax.ShapeDtypeStruct((D_MODEL, D_FF), BF16)] * 2 + [jax.ShapeDtypeStruct((D_FF, D_MODEL), BF16)]
    out_specs += [pl.BlockSpec((D_MODEL, ck), lambda c: (0, c))] * 2 + [pl.BlockSpec((ck, D_MODEL), lambda c: (c, 0))]
    post_in = () if post == "none" else (post_g,)
    outs = pl.pallas_call(
        functools.partial(_ffn_step_kernel, post=post, n_chunks=n_chunks),
        out_shape=out_shape,
        grid=(n_chunks,),
        in_specs=[whole((rows, D_MODEL)), pl.BlockSpec((None, 1, D_MODEL), lambda c: (layer, 0, 0)),
                  pl.BlockSpec((None, D_MODEL, ck), lambda c: (layer, 0, c)),
                  pl.BlockSpec((None, D_MODEL, ck), lambda c: (layer, 0, c)),
                  pl.BlockSpec((None, ck, D_MODEL), lambda c: (layer, c, 0))]
        + [whole((1, D_MODEL))] * len(post_in),
        out_specs=out_specs,
        scratch_shapes=[pltpu.VMEM((rows, D_MODEL), BF16), pltpu.VMEM((rows, D_MODEL), F32)],
        compiler_params=_params(1, 0, rows),
        name=name,
    )(x, g, wg, wu, wd, *post_in)
    return outs[:-3], outs[-3:]


def _ffn(x, g, wg, wu, wd, post_g, post, name, decode=None, merge=None, emit_dilations=()):
    rows = x.shape[0]
    tm = min(FFN_MERGE_ROW_TILE if merge else FFN_ROW_TILE, rows)
    rider_in, rider_in_specs, rider_out_shapes, rider_out_specs, rider_bytes = [], [], [], [], 0
    if decode is not None:
        tm = rows // (decode.cols.shape[0] * QUADS)
        rider_in, rider_in_specs, rider_out_shapes, rider_out_specs, rider_bytes = _decode_job_specs(
            decode, lambda i: (i // QUADS, i % QUADS))
    row_spec = pl.BlockSpec((tm, D_MODEL), lambda i: (i, 0))
    vec_spec = _resident((1, D_MODEL))
    plan = _FfnPlan(pre="merge" if merge else "none", post=post, riders=0 if decode is None else 1,
                    merge_dilations=tuple(merge[3]) if merge else (), emit_dilations=tuple(emit_dilations))
    pre_in, pre_specs, scratch, extra_bytes = [], [], [], 0
    if merge:
        outs_g, lses_g, w_out, dilations = merge
        grp_specs = [pl.BlockSpec((d, tm // d, GROUP_WIDTH), lambda i: (0, i, 0)) for d in dilations]
        pre_in = list(outs_g) + list(lses_g) + [w_out]
        pre_specs = grp_specs * 2 + [_resident((GROUP_WIDTH, D_MODEL))]
        sub = min(tm, FFN_SUB_ROWS)
        scratch.append(pltpu.VMEM((tm // sub, 2 * N_GROUPS, GROUP_WIDTH // V7X_LANES, sub, V7X_LANES), F32))
        extra_bytes += tm * GROUP_WIDTH * 4 * (2 * 6 + 6 + 4)
    out_shape = [jax.ShapeDtypeStruct((rows, D_MODEL), F32)]
    out_specs = [row_spec]
    if post == "mixer_input":
        out_shape.append(jax.ShapeDtypeStruct((rows, D_MODEL), BF16))
        out_specs.append(row_spec)
    elif post == "attn_input":
        out_shape += [jax.ShapeDtypeStruct((d, rows // d, D_MODEL), BF16) for d in emit_dilations]
        out_specs += [pl.BlockSpec((d, tm // d, D_MODEL), lambda i: (0, i, 0)) for d in emit_dilations]
        sub = min(tm, FFN_SUB_ROWS)
        scratch.append(pltpu.VMEM((tm // sub, 2, D_MODEL // V7X_LANES, sub, V7X_LANES), F32))
        extra_bytes += tm * D_MODEL * (4 + 2 * 2 * len(emit_dilations))
    sub = min(tm, FFN_SUB_ROWS)
    weights = 3 * D_MODEL * D_FF * 2
    tiles = tm * D_MODEL * (4 * 2 + 4 * 2 + 2 * 2)
    temps = 2 * (sub * D_FF * (4 + 4 + 4 + 2) + sub * D_MODEL * 12)
    post_in = () if post == "none" else (post_g,)
    outs = pl.pallas_call(
        functools.partial(_ffn_kernel, plan=plan),
        out_shape=out_shape + rider_out_shapes,
        grid=(rows // tm,),
        in_specs=[row_spec, vec_spec, _resident((D_MODEL, D_FF)), _resident((D_MODEL, D_FF)),
                  _resident((D_FF, D_MODEL))] + pre_specs + [vec_spec] * len(post_in) + rider_in_specs,
        out_specs=out_specs + rider_out_specs,
        scratch_shapes=scratch,
        compiler_params=_params(1, weights + tiles + temps + rider_bytes + extra_bytes, rows),
        name=name,
    )(x, g, wg, wu, wd, *pre_in, *post_in, *rider_in)
    n_own = len(out_shape)
    return outs[:n_own], (_decode_job_results(outs[n_own:]) if decode is not None else None)


def _pool_tail(x, u, sums, counts, wgrp_ref, scale_ref, wout_ref):
    y = x
    for g, w in enumerate(POOL_WINDOWS):
        cols = slice(g * POOL_GROUP_DIM, (g + 1) * POOL_GROUP_DIM)
        z = (sums[g] / counts[g] - u[:, cols]).astype(BF16)
        z = jnp.dot(z, wgrp_ref[g], preferred_element_type=F32) * scale_ref[:, cols]
        y = y + jnp.dot(z.astype(BF16), wout_ref[cols, :], preferred_element_type=F32)
    return y


def _pool_prompt_kernel(x_ref, h_ref, win_ref, wgrp_ref, scale_ref, wout_ref,
                        xo_ref, tail_ref, hist_ref, *, tm):
    i = pl.program_id(0)
    hist = POOL_STATE_LEN + 1

    @pl.when(i == 0)
    def _():
        hist_ref[...] = jnp.zeros((hist, D_MODEL), F32)

    u = jnp.dot(h_ref[...], win_ref[...], preferred_element_type=F32)
    pos = i * tm + lax.broadcasted_iota(jnp.int32, (tm, 1), 0)
    sums, counts = [], []
    for g, w in enumerate(POOL_WINDOWS):
        cols = slice(g * POOL_GROUP_DIM, (g + 1) * POOL_GROUP_DIM)
        s = jnp.concatenate([hist_ref[:, cols], u[:, cols]], axis=0)
        k = 1
        while k < w:
            s = s + pltpu.roll(s, k, 0)
            k *= 2
        sums.append(s[hist:, :])
        counts.append(jnp.minimum(w, pos + 1).astype(F32))
    xo_ref[...] = _pool_tail(x_ref[...], u, sums, counts, wgrp_ref, scale_ref, wout_ref)
    last = u[tm - hist:, :]
    tail_ref[...] = last
    hist_ref[...] = last


def _pool_prompt(x, h, win, wgrp, scale, wout):
    rows = x.shape[0]
    tm = ROW_TILE
    hist = POOL_STATE_LEN + 1
    row_spec = pl.BlockSpec((tm, D_MODEL), lambda i: (i, 0))
    weights = (2 * D_MODEL * D_MODEL + 4 * POOL_GROUP_DIM * POOL_GROUP_DIM) * 2
    tiles = tm * D_MODEL * (4 * 2 + 2 * 2 + 4 * 2) + (tm + hist) * D_MODEL * 4
    temps = tm * D_MODEL * 4 * 6
    return pl.pallas_call(
        functools.partial(_pool_prompt_kernel, tm=tm),
        out_shape=[jax.ShapeDtypeStruct((rows, D_MODEL), F32),
                   jax.ShapeDtypeStruct((hist, D_MODEL), F32)],
        grid=(rows // tm,),
        in_specs=[row_spec, row_spec, _resident((D_MODEL, D_MODEL)),
                  _resident((4, POOL_GROUP_DIM, POOL_GROUP_DIM)), _resident((1, D_MODEL)),
                  _resident((D_MODEL, D_MODEL))],
        out_specs=[row_spec, pl.BlockSpec((hist, D_MODEL), lambda i: (0, 0))],
        scratch_shapes=[pltpu.VMEM((hist, D_MODEL), F32)],
        compiler_params=_params(1, weights + tiles + temps),
        name="pool_prompt",
    )(x, h, win, wgrp, scale, wout)


def _pool_step_kernel(x_ref, h_ref, st_ref, win32_ref, wgrp32_ref, scale_ref, wout32_ref,
                      xo_ref, sto_ref, win_ref, wgrp_ref, wout_ref, *, past_len):
    win_ref[...] = win32_ref[...].astype(BF16)
    wgrp_ref[...] = wgrp32_ref[...].astype(BF16)
    wout_ref[...] = wout32_ref[...].astype(BF16)
    u = jnp.dot(h_ref[...], win_ref[...], preferred_element_type=F32)
    sums, counts = [], []
    for g, w in enumerate(POOL_WINDOWS):
        cols = slice(g * POOL_GROUP_DIM, (g + 1) * POOL_GROUP_DIM)
        acc = u[:, cols]
        for j in range(1, w):
            acc = acc + st_ref[POOL_STATE_LEN - j, :, cols]
        sums.append(acc)
        counts.append(float(min(w, past_len + 1)))
    xo_ref[...] = _pool_tail(x_ref[...], u, sums, counts, wgrp_ref, scale_ref, wout_ref)
    for k in range(POOL_STATE_LEN - 1):
        sto_ref[k] = st_ref[k + 1]
    sto_ref[POOL_STATE_LEN - 1] = u


def _pool_step(x, h, state, win, wgrp, scale, wout, past_len):
    rows = x.shape[0]
    full = lambda shape: pl.BlockSpec(shape, lambda i: (0,) * len(shape))
    weight_shapes = [(D_MODEL, D_MODEL), (4, POOL_GROUP_DIM, POOL_GROUP_DIM), (D_MODEL, D_MODEL)]
    outs = pl.pallas_call(
        functools.partial(_pool_step_kernel, past_len=past_len),
        out_shape=[jax.ShapeDtypeStruct((rows, D_MODEL), F32), jax.ShapeDtypeStruct(state.shape, F32)]
        + [jax.ShapeDtypeStruct(s, BF16) for s in weight_shapes],
        grid=(1,),
        in_specs=[full((rows, D_MODEL)), full((rows, D_MODEL)), full(state.shape),
                  full(weight_shapes[0]), full(weight_shapes[1]), full((1, D_MODEL)), full(weight_shapes[2])],
        out_specs=[full((rows, D_MODEL)), full(state.shape)] + [full(s) for s in weight_shapes],
        compiler_params=_params(1, 0, rows),
        name="pool_step",
    )(x, h, state, win, wgrp, scale, wout)
    return outs[0], outs[1], outs[2:]


def _qkv_kernel(*refs, q_scale):
    hs, w_ref, outs = refs[:N_GROUPS], refs[N_GROUPS], refs[N_GROUPS + 1:]
    width = 3 * GROUP_WIDTH
    for g, (h_ref, o_ref) in enumerate(zip(hs, outs)):
        d, sub, _ = h_ref.shape
        h = h_ref[...].reshape(d * sub, D_MODEL)
        res = jnp.dot(h, w_ref[:, g * width:(g + 1) * width], preferred_element_type=F32)
        for r in range(d):
            rows = slice(r * sub, (r + 1) * sub)
            o_ref[r, :, 0:GROUP_WIDTH] = (res[rows, 0:GROUP_WIDTH] * q_scale).astype(o_ref.dtype)
            o_ref[r, :, GROUP_WIDTH:] = res[rows, GROUP_WIDTH:].astype(o_ref.dtype)


def _qkv(hs, w, q_scale, dtype, name):
    rows = hs[0].shape[0] * hs[0].shape[1]
    tm = min(ROW_TILE, rows)
    width = 3 * GROUP_WIDTH
    weights = D_MODEL * QKV_WIDTH * 2
    tiles = N_GROUPS * tm * D_MODEL * 2 * 2 + tm * QKV_WIDTH * jnp.dtype(dtype).itemsize * 2
    temps = tm * width * 4 * 2
    return pl.pallas_call(
        functools.partial(_qkv_kernel, q_scale=q_scale),
        out_shape=[jax.ShapeDtypeStruct(h.shape[:2] + (width,), dtype) for h in hs],
        grid=(rows // tm,),
        in_specs=[pl.BlockSpec((h.shape[0], tm // h.shape[0], D_MODEL), lambda i: (0, i, 0)) for h in hs]
        + [_resident((D_MODEL, QKV_WIDTH))],
        out_specs=[pl.BlockSpec((h.shape[0], tm // h.shape[0], width), lambda i: (0, i, 0)) for h in hs],
        compiler_params=_params(1, weights + tiles + temps, rows),
        name=name,
    )(*hs, w)


def _qkv_step_kernel(h_ref, w32_ref, o_ref, w_ref):
    w = w32_ref[...].astype(BF16)
    w_ref[...] = w
    o_ref[...] = jnp.dot(h_ref[...], w, preferred_element_type=F32)


def _qkv_step(h, w):
    rows = h.shape[0]
    ck = 3 * GROUP_WIDTH // 2
    return pl.pallas_call(
        _qkv_step_kernel,
        out_shape=[jax.ShapeDtypeStruct((rows, QKV_WIDTH), F32), jax.ShapeDtypeStruct((D_MODEL, QKV_WIDTH), BF16)],
        grid=(QKV_WIDTH // ck,),
        in_specs=[pl.BlockSpec((rows, D_MODEL), lambda c: (0, 0)), pl.BlockSpec((D_MODEL, ck), lambda c: (0, c))],
        out_specs=[pl.BlockSpec((rows, ck), lambda c: (0, c)), pl.BlockSpec((D_MODEL, ck), lambda c: (0, c))],
        compiler_params=_params(1, 0, rows),
        name="qkv_step",
    )(h, w)


def _kv_tail_kernel(h_ref, wk_ref, wv_ref, o_ref):
    h = h_ref[...]
    o_ref[0] = jnp.dot(h, wk_ref[...], preferred_element_type=F32).T
    o_ref[1] = jnp.dot(h, wv_ref[...], preferred_element_type=F32).T


def _kv_tail(h, w, g, window):
    rows = h.shape[0]
    tm = min(window, 256)
    first = (rows - window) // tm
    col = 3 * g
    return pl.pallas_call(
        _kv_tail_kernel,
        out_shape=jax.ShapeDtypeStruct((2, GROUP_WIDTH, window), F32),
        grid=(window // tm,),
        in_specs=[pl.BlockSpec((tm, D_MODEL), lambda i: (first + i, 0)),
                  pl.BlockSpec((D_MODEL, GROUP_WIDTH), lambda i: (0, col + 1)),
                  pl.BlockSpec((D_MODEL, GROUP_WIDTH), lambda i: (0, col + 2))],
        out_specs=pl.BlockSpec((2, GROUP_WIDTH, tm), lambda i: (0, 0, i)),
        compiler_params=_params(1, 16 * 1024 * 1024),
        name="kv_tail_g%d" % g,
    )(h, w, w)


def _bucket_table():
    out = np.zeros((N_GROUPS, N_KEYS), np.int32)
    for g, d in enumerate(ATTN_DILATIONS):
        dist = np.arange(N_KEYS, dtype=np.int32) * d
        distf = np.maximum(dist, 1).astype(np.float32)
        log_b = MAX_EXACT + (np.log(distf / np.float32(MAX_EXACT)) / np.float32(math.log(MAX_DISTANCE / MAX_EXACT))
                             * np.float32(N_BUCKETS - MAX_EXACT)).astype(np.int32)
        log_b = np.minimum(log_b, N_BUCKETS - 1)
        out[g] = np.where(dist < MAX_EXACT, dist, log_b)
    return out


def _band_offsets():
    a = np.arange(Q_BLOCK)[:, None]
    b = np.arange(2 * Q_BLOCK)[None, :]
    j = Q_BLOCK + a - b
    return np.where((j >= 0) & (j <= Q_BLOCK), j, -1).astype(np.int32)


def _bias_kernel(tab_ref, bidx_ref, l0_ref, l1_ref, l2_ref, band_ref, s0_ref, s1_ref, s2_ref, new_ref, *, buckets):
    sub = lax.broadcasted_iota(jnp.int32, (HEADS, V7X_LANES), 0)
    for g, (lidx_ref, step_ref) in enumerate(((l0_ref, s0_ref), (l1_ref, s1_ref), (l2_ref, s2_ref))):
        bidx = bidx_ref[g]
        lidx = lidx_ref[...]
        used = sorted(set(int(v) for v in buckets[g]))
        t = jnp.zeros((HEADS, V7X_LANES), F32)
        for h in range(HEADS):
            t = jnp.where(sub == h, tab_ref[int(buckets[g][0]), g * HEADS + h], t)
        new_ref[g] = t
        for h in range(HEADS):
            tile = jnp.full(bidx.shape, NEG_INF, F32)
            row = jnp.full(lidx.shape, NEG_INF, F32)
            for v in used:
                tile = jnp.where(bidx == v, tab_ref[v, g * HEADS + h], tile)
                row = jnp.where(lidx == v, tab_ref[v, g * HEADS + h], row)
            band_ref[g, h] = tile * LOG2E
            step_ref[h:h + 1, :] = row


def _bias_tables(rel_bias):
    buckets = _bucket_table()
    band = _band_offsets()
    bidx = np.stack([np.where(band >= 0, buckets[g][np.maximum(band, 0)], -1) for g in range(N_GROUPS)])
    lidx = []
    for g, (w, d) in enumerate(zip(ATTN_WINDOWS, ATTN_DILATIONS)):
        pos = np.arange(w)
        lidx.append(np.where(pos % d == 0, buckets[g][(w - pos) // d], -1).astype(np.int32)[None])
    vmem = pl.BlockSpec(memory_space=pltpu.VMEM)
    return pl.pallas_call(
        functools.partial(_bias_kernel, buckets=buckets),
        out_shape=[jax.ShapeDtypeStruct((N_GROUPS, HEADS, Q_BLOCK, 2 * Q_BLOCK), F32)]
        + [jax.ShapeDtypeStruct((HEADS, w), F32) for w in ATTN_WINDOWS]
        + [jax.ShapeDtypeStruct((N_GROUPS, HEADS, V7X_LANES), F32)],
        in_specs=[pl.BlockSpec(memory_space=pltpu.SMEM)] + [vmem] * 4,
        out_specs=[vmem] * 5,
        compiler_params=pltpu.CompilerParams(vmem_limit_bytes=_vmem_limit(24 * 1024 * 1024)),
        name="bias_tables",
    )(rel_bias, jnp.asarray(bidx), *[jnp.asarray(l) for l in lidx])


ATTN_BLOCKS_PER_STEP = 4


def _attn_prompt_kernel(q_ref, kp_ref, kc_ref, vp_ref, vc_ref, bias_ref, o_ref, lse_ref, *, nb):
    low = lax.broadcasted_iota(jnp.int32, (1, V7X_LANES), 1) < HEAD_DIM
    col = lax.broadcasted_iota(jnp.int32, (1, 2 * Q_BLOCK), 1)
    no_prev = jnp.where(jnp.logical_and(pl.program_id(1) == 0, col < Q_BLOCK), NEG_INF, 0.0)
    q = q_ref[...]
    k = jnp.concatenate([kp_ref[...], kc_ref[...]], axis=0)
    v = jnp.concatenate([vp_ref[...], vc_ref[...]], axis=0)
    items = [(b, p, half) for b in range(nb) for p in range(HEADS // 2) for half in range(2)]
    q_rows = lambda b: slice(b * Q_BLOCK, (b + 1) * Q_BLOCK)
    k_rows = lambda b: slice(b * Q_BLOCK, (b + 2) * Q_BLOCK)
    lanes = lambda p: slice(p * V7X_LANES, (p + 1) * V7X_LANES)
    mine = lambda half: low if half == 0 else jnp.logical_not(low)

    scores = {}
    for b, p, half in items:
        qh = jnp.where(mine(half), q[q_rows(b), lanes(p)], jnp.zeros((), BF16))
        s = lax.dot_general(qh, k[k_rows(b), lanes(p)], (((1,), (1,)), ((), ())), preferred_element_type=F32)
        s = s + bias_ref[2 * p + half]
        scores[b, p, half] = s + no_prev if b == 0 else s
    tops = {it: jnp.max(scores[it], axis=-1, keepdims=True) for it in items}
    weights = {it: jnp.exp2(scores[it] - tops[it]).astype(BF16) for it in items}
    v_ones = {(p, half): jnp.where(mine(half), v[:, lanes(p)], jnp.ones((), BF16))
              for p in range(HEADS // 2) for half in range(2)}
    acc = {}
    for b, p, half in items:
        acc[b, p, half] = jnp.dot(weights[b, p, half], v_ones[p, half][k_rows(b), :],
                                  preferred_element_type=F32)
    for b in range(nb):
        for p in range(HEADS // 2):
            a0, a1 = acc[b, p, 0], acc[b, p, 1]
            den = pltpu.roll(jnp.where(low, a1, a0), HEAD_DIM, 1)
            o_ref[q_rows(b), lanes(p)] = jnp.where(low, a0, a1) * (1.0 / den)
            lse_ref[q_rows(b), lanes(p)] = jnp.where(low, tops[b, p, 0], tops[b, p, 1]) * LN2 + jnp.log(den)


def _attn_prompt(qkv, band_bias, g):
    d, rows, _ = qkv.shape
    nb = ATTN_BLOCKS_PER_STEP
    step_rows = nb * Q_BLOCK

    def spec(which, prev):
        if prev:
            return pl.BlockSpec((None, Q_BLOCK, GROUP_WIDTH), lambda r, n: (r, jnp.maximum(nb * n - 1, 0), which))
        return pl.BlockSpec((None, step_rows, GROUP_WIDTH), lambda r, n: (r, n, which))

    out_spec = pl.BlockSpec((None, step_rows, GROUP_WIDTH), lambda r, n: (r, n, 0))
    bias_spec = pl.BlockSpec((None, HEADS, Q_BLOCK, 2 * Q_BLOCK), lambda r, n: (g, 0, 0, 0))
    return pl.pallas_call(
        functools.partial(_attn_prompt_kernel, nb=nb),
        out_shape=[jax.ShapeDtypeStruct((d, rows, GROUP_WIDTH), F32)] * 2,
        grid=(d, rows // step_rows),
        in_specs=[spec(0, False), spec(1, True), spec(1, False), spec(2, True), spec(2, False), bias_spec],
        out_specs=[out_spec, out_spec],
        compiler_params=_params(2, 32 * 1024 * 1024),
        name="attn_prompt_g%d" % g,
    )(qkv, qkv, qkv, qkv, qkv, band_bias)


STEP_HEADS = 4


def _decode_attn_item(q_cols, kn_cols, vn_cols, kc_ref, vc_ref, bias_ref, bnew_ref, ko_ref, vo_ref):
    length = kc_ref.shape[-1]
    last_tile = slice(length - V7X_LANES, length)
    is_last = lax.broadcasted_iota(jnp.int32, (1, V7X_LANES), 1) == V7X_LANES - 1

    def rolled(ref, hh, new_col):
        x = pltpu.roll(ref[hh], length - 1, 1)
        return x[:, :length - V7X_LANES], jnp.where(is_last, new_col, x[:, last_tile])

    outs, lses = [], []
    for hh in range(STEP_HEADS):
        col = slice(hh, hh + 1)
        q = q_cols[:, col] * ATTN_SCALE
        k_new, v_new = kn_cols[:, col], vn_cols[:, col]
        s = jnp.sum(q * kc_ref[hh], axis=0, keepdims=True) + bias_ref[col, :]
        s_new = jnp.sum(q * k_new, axis=0, keepdims=True) + bnew_ref[col, 0:1]
        m = jnp.maximum(jnp.max(s, axis=-1, keepdims=True), s_new)
        e = jnp.exp(s - m)
        e_new = jnp.exp(s_new - m)
        den = jnp.sum(e, axis=-1, keepdims=True) + e_new
        acc = jnp.sum(e * vc_ref[hh], axis=-1, keepdims=True) + e_new * v_new
        outs.append(acc * (1.0 / den))
        lses.append(jnp.broadcast_to(m + jnp.log(den), (HEAD_DIM, 1)))
        for cache_ref, out_ref, new_col in ((kc_ref, ko_ref, k_new), (vc_ref, vo_ref, v_new)):
            body, tail = rolled(cache_ref, hh, new_col)
            if length > V7X_LANES:
                out_ref[hh, :, :length - V7X_LANES] = body
            out_ref[hh, :, last_tile] = tail
    return outs, lses


QUADS = HEADS // STEP_HEADS
DECODE_IN_LANES = N_GROUPS * 3 * STEP_HEADS
DECODE_OUT_LANES = N_GROUPS * 2 * STEP_HEADS


def _decode_attn_all(in_refs, out_refs):
    cols = in_refs[0][...]
    quad = lambda first: cols[:, first * STEP_HEADS:(first + 1) * STEP_HEADS]
    lane = lax.broadcasted_iota(jnp.int32, (1, DECODE_OUT_LANES), 1)
    packed = jnp.zeros((HEAD_DIM, DECODE_OUT_LANES), F32)
    for g in range(N_GROUPS):
        kc, vc, bias, bnew = in_refs[1 + 4 * g:5 + 4 * g]
        ko, vo = out_refs[1 + 2 * g:3 + 2 * g]
        outs, lses = _decode_attn_item(quad(3 * g), quad(3 * g + 1), quad(3 * g + 2), kc, vc, bias, bnew, ko, vo)
        for hh in range(STEP_HEADS):
            packed = jnp.where(lane == (2 * g) * STEP_HEADS + hh, outs[hh], packed)
            packed = jnp.where(lane == (2 * g + 1) * STEP_HEADS + hh, lses[hh], packed)
    out_refs[0][...] = packed


class _DecodeJob(NamedTuple):
    cols: jax.Array
    caches: tuple
    biases: tuple
    bias_new: jax.Array
    layer: int


def _decode_job(qkv_rows, cache_k, cache_v, step_bias, new_bias, layer):
    batch = qkv_rows.shape[0]
    cols = qkv_rows.reshape(batch, N_GROUPS * 3, QUADS, STEP_HEADS, HEAD_DIM)
    cols = jnp.transpose(cols, (0, 2, 4, 1, 3)).reshape(batch, QUADS, HEAD_DIM, DECODE_IN_LANES)
    quad = lambda c: c.reshape(c.shape[:2] + (QUADS, STEP_HEADS) + c.shape[3:])
    return _DecodeJob(cols, tuple((quad(k), quad(v)) for k, v in zip(cache_k, cache_v)),
                      tuple(b.reshape(QUADS, STEP_HEADS, -1) for b in step_bias),
                      new_bias.reshape(N_GROUPS, QUADS, STEP_HEADS, V7X_LANES), layer)


def _decode_job_specs(job, item_of_step):
    batch = job.cols.shape[0]
    item = lambda *idx: tuple(item_of_step(*idx))
    operands = [job.cols]
    in_specs = [pl.BlockSpec((None, None, HEAD_DIM, DECODE_IN_LANES), lambda *idx: item(*idx) + (0, 0))]
    out_shapes = [jax.ShapeDtypeStruct((batch, QUADS, HEAD_DIM, DECODE_OUT_LANES), F32)]
    out_specs = [pl.BlockSpec((None, None, HEAD_DIM, DECODE_OUT_LANES), lambda *idx: item(*idx) + (0, 0))]
    vmem = 0
    for g, ((kc, vc), bias) in enumerate(zip(job.caches, job.biases)):
        length = kc.shape[-1]
        block = (None, None, None, STEP_HEADS, HEAD_DIM, length)
        cache_in = pl.BlockSpec(block, lambda *idx: (job.layer,) + item(*idx) + (0, 0, 0))
        cache_out = pl.BlockSpec(block, lambda *idx: (0,) + item(*idx) + (0, 0, 0))
        operands += [kc, vc, bias, job.bias_new]
        in_specs += [cache_in, cache_in,
                     pl.BlockSpec((None, STEP_HEADS, length), lambda *idx: (item(*idx)[1], 0, 0)),
                     pl.BlockSpec((None, None, STEP_HEADS, V7X_LANES),
                                  functools.partial(lambda *idx, g: (g, item(*idx)[1], 0, 0), g=g))]
        out_shapes += [jax.ShapeDtypeStruct((1,) + kc.shape[1:], F32)] * 2
        out_specs += [cache_out, cache_out]
        vmem += 10 * STEP_HEADS * HEAD_DIM * length * 4
    return operands, in_specs, out_shapes, out_specs, vmem


def _decode_job_results(outs):
    cols = outs[0]
    batch = cols.shape[0]
    cols = cols.reshape(batch, QUADS, HEAD_DIM, N_GROUPS, 2, STEP_HEADS)
    rows = jnp.transpose(cols, (3, 4, 0, 1, 5, 2)).reshape(N_GROUPS, 2, 1, batch, GROUP_WIDTH)
    whole = lambda c: c.reshape(c.shape[:2] + (HEADS,) + c.shape[4:])
    return ([(rows[g, 0], rows[g, 1]) for g in range(N_GROUPS)],
            [(whole(outs[1 + 2 * g]), whole(outs[2 + 2 * g])) for g in range(N_GROUPS)])


def _merged_attention(refs, slab_ref, j, sub, dilations):
    n_slabs = GROUP_WIDTH // V7X_LANES

    def token_order(ref, d, slot):
        if d == 1:
            return ref[0, j * sub:(j + 1) * sub, :]
        part = slice(j * (sub // d), (j + 1) * (sub // d))
        for r in range(d):
            for c in range(n_slabs):
                slab_ref[slot, c, pl.ds(r, sub // d, stride=d), :] = ref[r, part, c * V7X_LANES:(c + 1) * V7X_LANES]
        return jnp.concatenate([slab_ref[slot, c] for c in range(n_slabs)], axis=1)

    os_ = [token_order(refs[g], d, g) for g, d in enumerate(dilations)]
    ls = [token_order(refs[N_GROUPS + g], d, N_GROUPS + g) for g, d in enumerate(dilations)]
    m = jnp.maximum(jnp.maximum(ls[0], ls[1]), ls[2])
    es = [jnp.exp(l - m) for l in ls]
    inv = 1.0 / (es[0] + es[1] + es[2])
    return es[0] * inv * os_[0] + es[1] * inv * os_[1] + es[2] * inv * os_[2]


def _merge_kernel(x_ref, *refs, tm, dilations):
    w_ref, xo_ref, slab_ref = refs[2 * N_GROUPS:]
    o = _merged_attention(refs[:2 * N_GROUPS], slab_ref, 0, tm, dilations)
    xo_ref[...] = x_ref[...] + jnp.dot(o.astype(BF16), w_ref[...], preferred_element_type=F32)


def _merge(x, outs, lses, w, dilations, name):
    rows = x.shape[0]
    tm = min(ROW_TILE, rows)
    row_spec = pl.BlockSpec((tm, D_MODEL), lambda i: (i, 0))
    grp_specs = [pl.BlockSpec((d, tm // d, GROUP_WIDTH), lambda i: (0, i, 0)) for d in dilations]
    tiles = tm * D_MODEL * 4 * 4 + tm * GROUP_WIDTH * 4 * (2 * 6 + 6) + GROUP_WIDTH * D_MODEL * 2
    temps = tm * GROUP_WIDTH * 4 * 8
    return pl.pallas_call(
        functools.partial(_merge_kernel, tm=tm, dilations=dilations),
        out_shape=jax.ShapeDtypeStruct((rows, D_MODEL), F32),
        grid=(rows // tm,),
        in_specs=[row_spec] + grp_specs * 2 + [_resident((GROUP_WIDTH, D_MODEL))],
        out_specs=row_spec,
        scratch_shapes=[pltpu.VMEM((2 * N_GROUPS, GROUP_WIDTH // V7X_LANES, tm, V7X_LANES), F32)],
        compiler_params=_params(1, tiles + temps, rows),
        name=name,
    )(x, *outs, *lses, w)


def kernel(x_prompt, x_sample, state_pool, cache_k_w128, cache_v_w128, cache_k_w512, cache_v_w512,
           cache_k_w2048, cache_v_w2048, ffn1_norm, ffn1_w_gate, ffn1_w_up, ffn1_w_down, mix_norm,
           pool_w_in, pool_w_group, pool_scale, pool_w_out, attn_w_qkv, attn_w_out, rel_bias,
           ffn2_norm, ffn2_w_gate, ffn2_w_up, ffn2_w_down, final_norm):
    seq = x_prompt.shape[1]
    batch = x_sample.shape[0]
    past_len = cache_k_w2048.shape[2]
    cache_k = (cache_k_w128, cache_k_w512, cache_k_w2048)
    cache_v = (cache_v_w128, cache_v_w512, cache_v_w2048)
    bf = lambda w: w.astype(BF16)
    vec = lambda v: v.reshape(1, D_MODEL)

    xp = x_prompt.reshape(seq, D_MODEL)
    xs = x_sample.reshape(batch, D_MODEL)
    band_bias, *step_bias, new_bias = _bias_tables(rel_bias)
    to_pos_minor = lambda c: jnp.transpose(c, (0, 1, 3, 4, 2))
    from_pos_minor = lambda c: jnp.transpose(c, (0, 1, 4, 2, 3))

    ffn_f32 = {1: (ffn1_norm[:, None], ffn1_w_gate, ffn1_w_up, ffn1_w_down),
               2: (ffn2_norm[:, None], ffn2_w_gate, ffn2_w_up, ffn2_w_down)}

    def ffn_step(x, which, layer, post_g, post, name):
        post_g = None if post_g is None else vec(post_g)
        return _ffn_step(x, *ffn_f32[which], layer, post_g, post, name)

    def ffn(x, which, layer, weights, post_g, post, name, **extra):
        post_g = None if post_g is None else vec(post_g)
        return _ffn(x, ffn_f32[which][0][layer], *weights, post_g, post, name, **extra)

    wout = bf(attn_w_out[0])
    no_dilation = (1,) * N_GROUPS

    (xs, hs), w_ffn = ffn_step(xs, 1, 0, mix_norm[0], "mixer_input", "ffn1_l0_step")
    (xp, hp), _ = ffn(xp, 1, 0, w_ffn, mix_norm[0], "mixer_input", "ffn1_l0_prompt")
    xs, pool_state_s, (w_in, w_grp, w_out) = _pool_step(xs, hs, jnp.swapaxes(state_pool[0], 0, 1), pool_w_in[0],
                                                      pool_w_group[0], vec(pool_scale[0]), pool_w_out[0], past_len)
    pool_state_s = jnp.swapaxes(pool_state_s, 0, 1)[None]
    xp, tail = _pool_prompt(xp, hp, w_in, w_grp, vec(pool_scale[0]), w_out)
    pool_state_p = tail[1:][None, None]

    (xs,), w_ffn2_l0 = ffn_step(xs, 2, 0, None, "none", "ffn2_l0_step")
    (xs, hs), w_ffn1_l1 = ffn_step(xs, 1, 1, mix_norm[1], "attn_input", "ffn1_l1_step")
    qkv_s, wqkv = _qkv_step(hs[0], attn_w_qkv[0])
    decode = _decode_job(qkv_s, [to_pos_minor(c) for c in cache_k], [to_pos_minor(c) for c in cache_v],
                         step_bias, new_bias, 0)

    (xp,), (decoded, rolled) = ffn(xp, 2, 0, w_ffn2_l0, None, "none", "ffn2_l0_prompt", decode=decode)

    (xp, *hp), _ = ffn(xp, 1, 1, w_ffn1_l1, mix_norm[1], "attn_input", "ffn1_l1_prompt",
                       emit_dilations=ATTN_DILATIONS)
    qkv_p = _qkv(hp, wqkv, ATTN_SCALE * LOG2E, BF16, "qkv_prompt")
    outs_p, lses_p, kv_prompt = [], [], []
    for g, w in enumerate(ATTN_WINDOWS):
        o, lse = _attn_prompt(qkv_p[g], band_bias, g)
        outs_p.append(o)
        lses_p.append(lse)
        keep = min(w, seq)
        kv = _kv_tail(hp[ATTN_DILATIONS.index(1)][0], wqkv, g, keep).reshape(2, 1, 1, HEADS, HEAD_DIM, keep)
        kv_prompt.append((from_pos_minor(kv[0]), from_pos_minor(kv[1])))

    xs = _merge(xs, [d[0] for d in decoded], [d[1] for d in decoded], wout, no_dilation, "merge_step")
    (ys,), w_ffn2_l1 = ffn_step(xs, 2, 1, final_norm, "final_norm", "ffn2_l1_step")
    rolled = [from_pos_minor(c) for pair in rolled for c in pair]

    (yp,), _ = ffn(xp, 2, 1, w_ffn2_l1, final_norm, "final_norm", "ffn2_l1_prompt",
                   merge=(outs_p, lses_p, wout, ATTN_DILATIONS))

    return (yp.reshape(x_prompt.shape), ys.reshape(x_sample.shape), pool_state_p, pool_state_s,
            kv_prompt[0][0], kv_prompt[0][1], rolled[0], rolled[1],
            kv_prompt[1][0], kv_prompt[1][1], rolled[2], rolled[3],
            kv_prompt[2][0], kv_prompt[2][1], rolled[4], rolled[5])
```

```python
import functools
import math
from typing import NamedTuple

import numpy as np
import jax
import jax.numpy as jnp
from jax import lax
from jax.experimental import pallas as pl
from jax.experimental.pallas import tpu as pltpu

F32 = jnp.float32
BF16 = jnp.bfloat16

D_MODEL = 1024
D_FF = 2816
RMS_EPS = 1e-6
POOL_WINDOWS = (2, 4, 8, 16)
POOL_GROUP_DIM = D_MODEL // len(POOL_WINDOWS)
POOL_STATE_LEN = max(POOL_WINDOWS) - 1
ATTN_WINDOWS = (128, 512, 2048)
ATTN_DILATIONS = (1, 4, 16)
N_GROUPS = 3
HEAD_DIM = 64
HEADS = 8
GROUP_WIDTH = HEADS * HEAD_DIM
QKV_WIDTH = N_GROUPS * 3 * GROUP_WIDTH
Q_BLOCK = 128
N_KEYS = Q_BLOCK + 1
ATTN_SCALE = HEAD_DIM ** -0.5
LOG2E = math.log2(math.e)
LN2 = math.log(2.0)
N_BUCKETS = 32
MAX_EXACT = N_BUCKETS // 2
MAX_DISTANCE = 2048
NEG_INF = -1e30

V7X_LANES = 128
V7X_VMEM_BYTES = 64 * 1024 * 1024
V7X_VMEM_RESERVE = 6 * 1024 * 1024
ROW_TILE = 1024
POOL_ROW_TILE = 1024
FFN_ROW_TILE = 1024
FFN_MERGE_ROW_TILE = 512
FFN_SUB_ROWS = 256
FFN_MERGE_CHAINS = (256, 256)
DECODE_MAX_ROWS = 128


def _vmem_limit(estimate_bytes):
    return int(min(V7X_VMEM_BYTES - V7X_VMEM_RESERVE, estimate_bytes))


def _params(n_axes, vmem_bytes, rows=None):
    if rows is not None and rows < DECODE_MAX_ROWS:
        vmem_bytes = V7X_VMEM_BYTES
    return pltpu.CompilerParams(
        dimension_semantics=("arbitrary",) * n_axes,
        vmem_limit_bytes=_vmem_limit(vmem_bytes))


def _resident(shape):
    return pl.BlockSpec(shape, lambda *_: (0,) * len(shape), pipeline_mode=pl.Buffered(1))


def _rms(x, g):
    return x * lax.rsqrt(jnp.mean(x * x, axis=-1, keepdims=True) + RMS_EPS) * g


class _FfnPlan(NamedTuple):
    pre: str
    post: str
    riders: int
    merge_dilations: tuple
    emit_dilations: tuple
    chains: tuple

    @property
    def n_pre(self):
        return 2 * N_GROUPS + 1 if self.pre == "merge" else 0

    @property
    def n_out(self):
        return {"none": 1, "mixer_input": 2, "final_norm": 1, "attn_input": 1 + len(self.emit_dilations)}[self.post]


def _emit_regrouped(h, out_refs, dilations, slab_ref, start):
    n_slabs = D_MODEL // V7X_LANES
    lanes = lambda c: slice(c * V7X_LANES, (c + 1) * V7X_LANES)
    classes = [(0, h)]
    prev = 1
    for level, (o_ref, d) in enumerate(zip(out_refs, dilations)):
        ratio = d // prev
        if ratio > 1:
            park = slab_ref.at[level % 2]
            finer = []
            for idx, (res, rows) in enumerate(classes):
                n = rows.shape[0]
                for c in range(n_slabs):
                    park[c, idx * n:(idx + 1) * n, :] = rows[:, lanes(c)]
                for q in range(ratio):
                    pick = [park[c, pl.ds(idx * n + q, n // ratio, stride=ratio), :] for c in range(n_slabs)]
                    finer.append((res + prev * q, jnp.concatenate(pick, axis=1)))
            classes, prev = finer, d
        for res, rows in classes:
            n = rows.shape[0]
            o_ref[res, start // d:start // d + n, :] = rows.astype(BF16)


def _ffn_kernel(x_ref, g_ref, wg_ref, wu_ref, wd_ref, *refs, plan):
    n_post = 0 if plan.post == "none" else 1
    n_rider_in, n_rider_out = (1 + 4 * N_GROUPS) * plan.riders, (1 + 2 * N_GROUPS) * plan.riders
    pos = 0
    pre_refs, pos = refs[pos:pos + plan.n_pre], pos + plan.n_pre
    pg_ref, pos = (refs[pos] if n_post else None), pos + n_post
    rider_in, pos = refs[pos:pos + n_rider_in], pos + n_rider_in
    out_refs, pos = refs[pos:pos + plan.n_out], pos + plan.n_out
    rider_out, pos = refs[pos:pos + n_rider_out], pos + n_rider_out
    scratch = refs[pos:]
    tm = x_ref.shape[0]

    if plan.riders:
        _decode_attn_all(rider_in, rider_out)

    assert sum(plan.chains) == tm
    start = 0
    for j, n in enumerate(plan.chains):
        rows = slice(start, start + n)
        x = x_ref[rows, :]
        if plan.pre == "merge":
            attn = _merged_attention(pre_refs[:2 * N_GROUPS], scratch[0].at[j], start, n, plan.merge_dilations)
            x = x + jnp.dot(attn.astype(BF16), pre_refs[-1][...], preferred_element_type=F32)
        h = _rms(x, g_ref[...]).astype(BF16)
        gate = jnp.dot(h, wg_ref[...], preferred_element_type=F32)
        up = jnp.dot(h, wu_ref[...], preferred_element_type=F32)
        a = (gate * jax.nn.sigmoid(gate) * up).astype(BF16)
        y = x + 0.5 * jnp.dot(a, wd_ref[...], preferred_element_type=F32)
        start += n
        if plan.post == "final_norm":
            out_refs[0][rows, :] = _rms(y, pg_ref[...])
            continue
        out_refs[0][rows, :] = y
        if plan.post == "mixer_input":
            out_refs[1][rows, :] = _rms(y, pg_ref[...]).astype(BF16)
        elif plan.post == "attn_input":
            _emit_regrouped(_rms(y, pg_ref[...]), out_refs[1:], plan.emit_dilations, scratch[-1].at[j], rows.start)


FFN_STEP_CHUNK = 1408


def _ffn_step_kernel(x_ref, g_ref, wg_ref, wu_ref, wd_ref, *refs, post, n_chunks):
    n_post = 0 if post == "none" else 1
    n_y = 1 if post in ("none", "final_norm") else 2
    pg_ref = refs[0] if n_post else None
    y_refs = refs[n_post:n_post + n_y]
    wgo_ref, wuo_ref, wdo_ref, h_ref, acc_ref = refs[n_post + n_y:]
    c = pl.program_id(0)

    @pl.when(c == 0)
    def _():
        h_ref[...] = _rms(x_ref[...], g_ref[...]).astype(BF16)
        acc_ref[...] = jnp.zeros(acc_ref.shape, F32)

    wg, wu, wd = wg_ref[...].astype(BF16), wu_ref[...].astype(BF16), wd_ref[...].astype(BF16)
    wgo_ref[...], wuo_ref[...], wdo_ref[...] = wg, wu, wd
    h = h_ref[...]
    gate = jnp.dot(h, wg, preferred_element_type=F32)
    up = jnp.dot(h, wu, preferred_element_type=F32)
    a = (gate * jax.nn.sigmoid(gate) * up).astype(BF16)
    acc_ref[...] += jnp.dot(a, wd, preferred_element_type=F32)

    @pl.when(c == n_chunks - 1)
    def _():
        y = x_ref[...] + 0.5 * acc_ref[...]
        if post == "final_norm":
            y_refs[0][...] = _rms(y, pg_ref[...])
        else:
            y_refs[0][...] = y
            if post == "mixer_input":
                y_refs[1][...] = _rms(y, pg_ref[...]).astype(BF16)
            elif post == "attn_input":
                y_refs[1][0] = _rms(y, pg_ref[...]).astype(BF16)


def _ffn_step(x, g, wg, wu, wd, layer, post_g, post, name):
    rows = x.shape[0]
    ck = FFN_STEP_CHUNK
    n_chunks = D_FF // ck
    whole = lambda shape: pl.BlockSpec(shape, lambda c: (0,) * len(shape))
    out_shape = [jax.ShapeDtypeStruct((rows, D_MODEL), F32)]
    out_specs = [whole((rows, D_MODEL))]
    if post == "mixer_input":
        out_shape.append(jax.ShapeDtypeStruct((rows, D_MODEL), BF16))
        out_specs.append(whole((rows, D_MODEL)))
    elif post == "attn_input":
        out_shape.append(jax.ShapeDtypeStruct((1, rows, D_MODEL), BF16))
        out_specs.append(whole((1, rows, D_MODEL)))
    out_shape += [jax.ShapeDtypeStruct((D_MODEL, D_FF), BF16)] * 2 + [jax.ShapeDtypeStruct((D_FF, D_MODEL), BF16)]
    out_specs += [pl.BlockSpec((D_MODEL, ck), lambda c: (0, c))] * 2 + [pl.BlockSpec((ck, D_MODEL), lambda c: (c, 0))]
    post_in = () if post == "none" else (post_g,)
    outs = pl.pallas_call(
        functools.partial(_ffn_step_kernel, post=post, n_chunks=n_chunks),
        out_shape=out_shape,
        grid=(n_chunks,),
        in_specs=[whole((rows, D_MODEL)), pl.BlockSpec((None, 1, D_MODEL), lambda c: (layer, 0, 0)),
                  pl.BlockSpec((None, D_MODEL, ck), lambda c: (layer, 0, c)),
                  pl.BlockSpec((None, D_MODEL, ck), lambda c: (layer, 0, c)),
                  pl.BlockSpec((None, ck, D_MODEL), lambda c: (layer, c, 0))]
        + [whole((1, D_MODEL))] * len(post_in),
        out_specs=out_specs,
        scratch_shapes=[pltpu.VMEM((rows, D_MODEL), BF16), pltpu.VMEM((rows, D_MODEL), F32)],
        compiler_params=_params(1, 0, rows),
        name=name,
    )(x, g, wg, wu, wd, *post_in)
    return outs[:-3], outs[-3:]


def _ffn(x, g, wg, wu, wd, post_g, post, name, decode=None, merge=None, emit_dilations=()):
    rows = x.shape[0]
    tm = min(FFN_MERGE_ROW_TILE if merge else FFN_ROW_TILE, rows)
    rider_in, rider_in_specs, rider_out_shapes, rider_out_specs, rider_bytes = [], [], [], [], 0
    if decode is not None:
        tm = rows // (decode.cols.shape[0] * QUADS)
        rider_in, rider_in_specs, rider_out_shapes, rider_out_specs, rider_bytes = _decode_job_specs(
            decode, lambda i: (i // QUADS, i % QUADS))
    row_spec = pl.BlockSpec((tm, D_MODEL), lambda i: (i, 0))
    vec_spec = _resident((1, D_MODEL))
    if merge:
        chains = tuple(FFN_MERGE_CHAINS)
    else:
        chains = (min(tm, FFN_SUB_ROWS),) * max(1, tm // FFN_SUB_ROWS)
    sub = max(chains)
    plan = _FfnPlan(pre="merge" if merge else "none", post=post, riders=0 if decode is None else 1,
                    merge_dilations=tuple(merge[3]) if merge else (), emit_dilations=tuple(emit_dilations),
                    chains=chains)
    pre_in, pre_specs, scratch, extra_bytes = [], [], [], 0
    if merge:
        outs_g, lses_g, w_out, dilations = merge
        grp_specs = [pl.BlockSpec((d, tm // d, GROUP_WIDTH), lambda i: (0, i, 0)) for d in dilations]
        pre_in = list(outs_g) + list(lses_g) + [w_out]
        pre_specs = grp_specs * 2 + [_resident((GROUP_WIDTH, D_MODEL))]
        scratch.append(pltpu.VMEM((len(chains), 2 * N_GROUPS, GROUP_WIDTH // V7X_LANES, sub, V7X_LANES), F32))
        extra_bytes += tm * GROUP_WIDTH * 4 * (2 * 6 + 6 + 4)
    out_shape = [jax.ShapeDtypeStruct((rows, D_MODEL), F32)]
    out_specs = [row_spec]
    if post == "mixer_input":
        out_shape.append(jax.ShapeDtypeStruct((rows, D_MODEL), BF16))
        out_specs.append(row_spec)
    elif post == "attn_input":
        out_shape += [jax.ShapeDtypeStruct((d, rows // d, D_MODEL), BF16) for d in emit_dilations]
        out_specs += [pl.BlockSpec((d, tm // d, D_MODEL), lambda i: (0, i, 0)) for d in emit_dilations]
        scratch.append(pltpu.VMEM((len(chains), 2, D_MODEL // V7X_LANES, sub, V7X_LANES), F32))
        extra_bytes += tm * D_MODEL * (4 + 2 * 2 * len(emit_dilations))
    weights = 3 * D_MODEL * D_FF * 2
    tiles = tm * D_MODEL * (4 * 2 + 4 * 2 + 2 * 2)
    temps = 2 * (sub * D_FF * (4 + 4 + 4 + 2) + sub * D_MODEL * 12)
    post_in = () if post == "none" else (post_g,)
    outs = pl.pallas_call(
        functools.partial(_ffn_kernel, plan=plan),
        out_shape=out_shape + rider_out_shapes,
        grid=(rows // tm,),
        in_specs=[row_spec, vec_spec, _resident((D_MODEL, D_FF)), _resident((D_MODEL, D_FF)),
                  _resident((D_FF, D_MODEL))] + pre_specs + [vec_spec] * len(post_in) + rider_in_specs,
        out_specs=out_specs + rider_out_specs,
        scratch_shapes=scratch,
        compiler_params=_params(1, weights + tiles + temps + rider_bytes + extra_bytes, rows),
        name=name,
    )(x, g, wg, wu, wd, *pre_in, *post_in, *rider_in)
    n_own = len(out_shape)
    return outs[:n_own], (_decode_job_results(outs[n_own:]) if decode is not None else None)


def _pool_tail(x, u, sums, counts, wgrp_ref, scale_ref, wout_ref):
    y = x
    for g, w in enumerate(POOL_WINDOWS):
        cols = slice(g * POOL_GROUP_DIM, (g + 1) * POOL_GROUP_DIM)
        z = (sums[g] / counts[g] - u[:, cols]).astype(BF16)
        z = jnp.dot(z, wgrp_ref[g], preferred_element_type=F32) * scale_ref[:, cols]
        y = y + jnp.dot(z.astype(BF16), wout_ref[cols, :], preferred_element_type=F32)
    return y


def _pool_prompt_kernel(x_ref, h_ref, win_ref, wgrp_ref, scale_ref, wout_ref,
                        xo_ref, tail_ref, hist_ref, *, tm):
    i = pl.program_id(0)
    hist = POOL_STATE_LEN + 1

    @pl.when(i == 0)
    def _():
        hist_ref[...] = jnp.zeros((hist, D_MODEL), F32)

    u = jnp.dot(h_ref[...], win_ref[...], preferred_element_type=F32)
    pos = i * tm + lax.broadcasted_iota(jnp.int32, (tm, 1), 0)
    sums, counts = [], []
    for g, w in enumerate(POOL_WINDOWS):
        cols = slice(g * POOL_GROUP_DIM, (g + 1) * POOL_GROUP_DIM)
        s = jnp.concatenate([hist_ref[:, cols], u[:, cols]], axis=0)
        k = 1
        while k < w:
            s = s + pltpu.roll(s, k, 0)
            k *= 2
        sums.append(s[hist:, :])
        counts.append(jnp.minimum(w, pos + 1).astype(F32))
    xo_ref[...] = _pool_tail(x_ref[...], u, sums, counts, wgrp_ref, scale_ref, wout_ref)
    last = u[tm - hist:, :]
    tail_ref[...] = last
    hist_ref[...] = last


def _pool_prompt(x, h, win, wgrp, scale, wout):
    rows = x.shape[0]
    tm = POOL_ROW_TILE
    hist = POOL_STATE_LEN + 1
    row_spec = pl.BlockSpec((tm, D_MODEL), lambda i: (i, 0))
    weights = (2 * D_MODEL * D_MODEL + 4 * POOL_GROUP_DIM * POOL_GROUP_DIM) * 2
    tiles = tm * D_MODEL * (4 * 2 + 2 * 2 + 4 * 2) + (tm + hist) * D_MODEL * 4
    temps = tm * D_MODEL * 4 * 6
    return pl.pallas_call(
        functools.partial(_pool_prompt_kernel, tm=tm),
        out_shape=[jax.ShapeDtypeStruct((rows, D_MODEL), F32),
                   jax.ShapeDtypeStruct((hist, D_MODEL), F32)],
        grid=(rows // tm,),
        in_specs=[row_spec, row_spec, _resident((D_MODEL, D_MODEL)),
                  _resident((4, POOL_GROUP_DIM, POOL_GROUP_DIM)), _resident((1, D_MODEL)),
                  _resident((D_MODEL, D_MODEL))],
        out_specs=[row_spec, pl.BlockSpec((hist, D_MODEL), lambda i: (0, 0))],
        scratch_shapes=[pltpu.VMEM((hist, D_MODEL), F32)],
        compiler_params=_params(1, weights + tiles + temps),
        name="pool_prompt",
    )(x, h, win, wgrp, scale, wout)


def _pool_step_kernel(x_ref, h_ref, st_ref, win32_ref, wgrp32_ref, scale_ref, wout32_ref,
                      xo_ref, sto_ref, win_ref, wgrp_ref, wout_ref, *, past_len):
    win_ref[...] = win32_ref[...].astype(BF16)
    wgrp_ref[...] = wgrp32_ref[...].astype(BF16)
    wout_ref[...] = wout32_ref[...].astype(BF16)
    u = jnp.dot(h_ref[...], win_ref[...], preferred_element_type=F32)
    sums, counts = [], []
    for g, w in enumerate(POOL_WINDOWS):
        cols = slice(g * POOL_GROUP_DIM, (g + 1) * POOL_GROUP_DIM)
        acc = u[:, cols]
        for j in range(1, w):
            acc = acc + st_ref[POOL_STATE_LEN - j, :, cols]
        sums.append(acc)
        counts.append(float(min(w, past_len + 1)))
    xo_ref[...] = _pool_tail(x_ref[...], u, sums, counts, wgrp_ref, scale_ref, wout_ref)
    for k in range(POOL_STATE_LEN - 1):
        sto_ref[k] = st_ref[k + 1]
    sto_ref[POOL_STATE_LEN - 1] = u


def _pool_step(x, h, state, win, wgrp, scale, wout, past_len):
    rows = x.shape[0]
    full = lambda shape: pl.BlockSpec(shape, lambda i: (0,) * len(shape))
    weight_shapes = [(D_MODEL, D_MODEL), (4, POOL_GROUP_DIM, POOL_GROUP_DIM), (D_MODEL, D_MODEL)]
    outs = pl.pallas_call(
        functools.partial(_pool_step_kernel, past_len=past_len),
        out_shape=[jax.ShapeDtypeStruct((rows, D_MODEL), F32), jax.ShapeDtypeStruct(state.shape, F32)]
        + [jax.ShapeDtypeStruct(s, BF16) for s in weight_shapes],
        grid=(1,),
        in_specs=[full((rows, D_MODEL)), full((rows, D_MODEL)), full(state.shape),
                  full(weight_shapes[0]), full(weight_shapes[1]), full((1, D_MODEL)), full(weight_shapes[2])],
        out_specs=[full((rows, D_MODEL)), full(state.shape)] + [full(s) for s in weight_shapes],
        compiler_params=_params(1, 0, rows),
        name="pool_step",
    )(x, h, state, win, wgrp, scale, wout)
    return outs[0], outs[1], outs[2:]


def _qkv_kernel(*refs, q_scale):
    hs, w_ref, outs = refs[:N_GROUPS], refs[N_GROUPS], refs[N_GROUPS + 1:]
    width = 3 * GROUP_WIDTH
    for g, (h_ref, o_ref) in enumerate(zip(hs, outs)):
        d, sub, _ = h_ref.shape
        h = h_ref[...].reshape(d * sub, D_MODEL)
        res = jnp.dot(h, w_ref[:, g * width:(g + 1) * width], preferred_element_type=F32)
        for r in range(d):
            rows = slice(r * sub, (r + 1) * sub)
            o_ref[r, :, 0:GROUP_WIDTH] = (res[rows, 0:GROUP_WIDTH] * q_scale).astype(o_ref.dtype)
            o_ref[r, :, GROUP_WIDTH:] = res[rows, GROUP_WIDTH:].astype(o_ref.dtype)


def _qkv(hs, w, q_scale, dtype, name):
    rows = hs[0].shape[0] * hs[0].shape[1]
    tm = min(ROW_TILE, rows)
    width = 3 * GROUP_WIDTH
    weights = D_MODEL * QKV_WIDTH * 2
    tiles = N_GROUPS * tm * D_MODEL * 2 * 2 + tm * QKV_WIDTH * jnp.dtype(dtype).itemsize * 2
    temps = tm * width * 4 * 2
    return pl.pallas_call(
        functools.partial(_qkv_kernel, q_scale=q_scale),
        out_shape=[jax.ShapeDtypeStruct(h.shape[:2] + (width,), dtype) for h in hs],
        grid=(rows // tm,),
        in_specs=[pl.BlockSpec((h.shape[0], tm // h.shape[0], D_MODEL), lambda i: (0, i, 0)) for h in hs]
        + [_resident((D_MODEL, QKV_WIDTH))],
        out_specs=[pl.BlockSpec((h.shape[0], tm // h.shape[0], width), lambda i: (0, i, 0)) for h in hs],
        compiler_params=_params(1, weights + tiles + temps, rows),
        name=name,
    )(*hs, w)


def _qkv_step_kernel(h_ref, w32_ref, o_ref, w_ref):
    w = w32_ref[...].astype(BF16)
    w_ref[...] = w
    o_ref[...] = jnp.dot(h_ref[...], w, preferred_element_type=F32)


def _qkv_step(h, w):
    rows = h.shape[0]
    ck = 3 * GROUP_WIDTH
    return pl.pallas_call(
        _qkv_step_kernel,
        out_shape=[jax.ShapeDtypeStruct((rows, QKV_WIDTH), F32), jax.ShapeDtypeStruct((D_MODEL, QKV_WIDTH), BF16)],
        grid=(QKV_WIDTH // ck,),
        in_specs=[pl.BlockSpec((rows, D_MODEL), lambda c: (0, 0)), pl.BlockSpec((D_MODEL, ck), lambda c: (0, c))],
        out_specs=[pl.BlockSpec((rows, ck), lambda c: (0, c)), pl.BlockSpec((D_MODEL, ck), lambda c: (0, c))],
        compiler_params=_params(1, 0, rows),
        name="qkv_step",
    )(h, w)


def _kv_tail_kernel(h_ref, wk_ref, wv_ref, o_ref):
    h = h_ref[...]
    o_ref[0] = jnp.dot(h, wk_ref[...], preferred_element_type=F32).T
    o_ref[1] = jnp.dot(h, wv_ref[...], preferred_element_type=F32).T


def _kv_tail(h, w, g, window):
    rows = h.shape[0]
    tm = min(window, 256)
    first = (rows - window) // tm
    col = 3 * g
    blocks = 2 * (tm * D_MODEL * 2 + 2 * D_MODEL * GROUP_WIDTH * 2 + 2 * GROUP_WIDTH * tm * 4)
    temps = 4 * tm * GROUP_WIDTH * 4
    return pl.pallas_call(
        _kv_tail_kernel,
        out_shape=jax.ShapeDtypeStruct((2, GROUP_WIDTH, window), F32),
        grid=(window // tm,),
        in_specs=[pl.BlockSpec((tm, D_MODEL), lambda i: (first + i, 0)),
                  pl.BlockSpec((D_MODEL, GROUP_WIDTH), lambda i: (0, col + 1)),
                  pl.BlockSpec((D_MODEL, GROUP_WIDTH), lambda i: (0, col + 2))],
        out_specs=pl.BlockSpec((2, GROUP_WIDTH, tm), lambda i: (0, 0, i)),
        compiler_params=_params(1, blocks + temps),
        name="kv_tail_g%d" % g,
    )(h, w, w)


def _bucket_table():
    out = np.zeros((N_GROUPS, N_KEYS), np.int32)
    for g, d in enumerate(ATTN_DILATIONS):
        dist = np.arange(N_KEYS, dtype=np.int32) * d
        distf = np.maximum(dist, 1).astype(np.float32)
        log_b = MAX_EXACT + (np.log(distf / np.float32(MAX_EXACT)) / np.float32(math.log(MAX_DISTANCE / MAX_EXACT))
                             * np.float32(N_BUCKETS - MAX_EXACT)).astype(np.int32)
        log_b = np.minimum(log_b, N_BUCKETS - 1)
        out[g] = np.where(dist < MAX_EXACT, dist, log_b)
    return out


def _band_offsets():
    a = np.arange(Q_BLOCK)[:, None]
    b = np.arange(2 * Q_BLOCK)[None, :]
    j = Q_BLOCK + a - b
    return np.where((j >= 0) & (j <= Q_BLOCK), j, -1).astype(np.int32)


def _bias_kernel(tab_ref, bidx_ref, l0_ref, l1_ref, l2_ref, band_ref, s0_ref, s1_ref, s2_ref, new_ref, *, buckets):
    sub = lax.broadcasted_iota(jnp.int32, (HEADS, V7X_LANES), 0)
    for g, (lidx_ref, step_ref) in enumerate(((l0_ref, s0_ref), (l1_ref, s1_ref), (l2_ref, s2_ref))):
        bidx = bidx_ref[g]
        lidx = lidx_ref[...]
        used = sorted(set(int(v) for v in buckets[g]))
        t = jnp.zeros((HEADS, V7X_LANES), F32)
        for h in range(HEADS):
            t = jnp.where(sub == h, tab_ref[int(buckets[g][0]), g * HEADS + h], t)
        new_ref[g] = t
        for h in range(HEADS):
            tile = jnp.full(bidx.shape, NEG_INF, F32)
            row = jnp.full(lidx.shape, NEG_INF, F32)
            for v in used:
                tile = jnp.where(bidx == v, tab_ref[v, g * HEADS + h], tile)
                row = jnp.where(lidx == v, tab_ref[v, g * HEADS + h], row)
            band_ref[g, h] = tile * LOG2E
            step_ref[h:h + 1, :] = row


def _bias_tables(rel_bias):
    buckets = _bucket_table()
    band = _band_offsets()
    bidx = np.stack([np.where(band >= 0, buckets[g][np.maximum(band, 0)], -1) for g in range(N_GROUPS)])
    lidx = []
    for g, (w, d) in enumerate(zip(ATTN_WINDOWS, ATTN_DILATIONS)):
        pos = np.arange(w)
        lidx.append(np.where(pos % d == 0, buckets[g][(w - pos) // d], -1).astype(np.int32)[None])
    vmem = pl.BlockSpec(memory_space=pltpu.VMEM)
    tile_bytes = N_GROUPS * HEADS * Q_BLOCK * 2 * Q_BLOCK * 4
    return pl.pallas_call(
        functools.partial(_bias_kernel, buckets=buckets),
        out_shape=[jax.ShapeDtypeStruct((N_GROUPS, HEADS, Q_BLOCK, 2 * Q_BLOCK), F32)]
        + [jax.ShapeDtypeStruct((HEADS, w), F32) for w in ATTN_WINDOWS]
        + [jax.ShapeDtypeStruct((N_GROUPS, HEADS, V7X_LANES), F32)],
        in_specs=[pl.BlockSpec(memory_space=pltpu.SMEM)] + [vmem] * 4,
        out_specs=[vmem] * 5,
        compiler_params=pltpu.CompilerParams(vmem_limit_bytes=_vmem_limit(4 * tile_bytes)),
        name="bias_tables",
    )(rel_bias, jnp.asarray(bidx), *[jnp.asarray(l) for l in lidx])


ATTN_BLOCKS_PER_STEP = 8


def _attn_prompt_kernel(q_ref, kp_ref, kc_ref, vp_ref, vc_ref, bias_ref, o_ref, lse_ref, *, nb):
    low = lax.broadcasted_iota(jnp.int32, (1, V7X_LANES), 1) < HEAD_DIM
    col = lax.broadcasted_iota(jnp.int32, (1, 2 * Q_BLOCK), 1)
    no_prev = jnp.where(jnp.logical_and(pl.program_id(1) == 0, col < Q_BLOCK), NEG_INF, 0.0)
    q = q_ref[...]
    k = jnp.concatenate([kp_ref[...], kc_ref[...]], axis=0)
    v = jnp.concatenate([vp_ref[...], vc_ref[...]], axis=0)
    items = [(b, p, half) for b in range(nb) for p in range(HEADS // 2) for half in range(2)]
    q_rows = lambda b: slice(b * Q_BLOCK, (b + 1) * Q_BLOCK)
    k_rows = lambda b: slice(b * Q_BLOCK, (b + 2) * Q_BLOCK)
    lanes = lambda p: slice(p * V7X_LANES, (p + 1) * V7X_LANES)
    mine = lambda half: low if half == 0 else jnp.logical_not(low)

    scores = {}
    for b, p, half in items:
        qh = jnp.where(mine(half), q[q_rows(b), lanes(p)], jnp.zeros((), BF16))
        s = lax.dot_general(qh, k[k_rows(b), lanes(p)], (((1,), (1,)), ((), ())), preferred_element_type=F32)
        s = s + bias_ref[2 * p + half]
        scores[b, p, half] = s + no_prev if b == 0 else s
    tops = {it: jnp.max(scores[it], axis=-1, keepdims=True) for it in items}
    weights = {it: jnp.exp2(scores[it] - tops[it]).astype(BF16) for it in items}
    v_ones = {(p, half): jnp.where(mine(half), v[:, lanes(p)], jnp.ones((), BF16))
              for p in range(HEADS // 2) for half in range(2)}
    acc = {}
    for b, p, half in items:
        acc[b, p, half] = jnp.dot(weights[b, p, half], v_ones[p, half][k_rows(b), :],
                                  preferred_element_type=F32)
    for b in range(nb):
        for p in range(HEADS // 2):
            a0, a1 = acc[b, p, 0], acc[b, p, 1]
            den = pltpu.roll(jnp.where(low, a1, a0), HEAD_DIM, 1)
            o_ref[q_rows(b), lanes(p)] = jnp.where(low, a0, a1) * (1.0 / den)
            lse_ref[q_rows(b), lanes(p)] = jnp.where(low, tops[b, p, 0], tops[b, p, 1]) * LN2 + jnp.log(den)


def _attn_prompt(qkv, band_bias, g):
    d, rows, _ = qkv.shape
    nb = ATTN_BLOCKS_PER_STEP
    step_rows = nb * Q_BLOCK

    def spec(which, prev):
        if prev:
            return pl.BlockSpec((None, Q_BLOCK, GROUP_WIDTH), lambda r, n: (r, jnp.maximum(nb * n - 1, 0), which))
        return pl.BlockSpec((None, step_rows, GROUP_WIDTH), lambda r, n: (r, n, which))

    out_spec = pl.BlockSpec((None, step_rows, GROUP_WIDTH), lambda r, n: (r, n, 0))
    bias_spec = pl.BlockSpec((None, HEADS, Q_BLOCK, 2 * Q_BLOCK), lambda r, n: (g, 0, 0, 0))
    blocks = 2 * ((3 * step_rows + 2 * Q_BLOCK) * GROUP_WIDTH * 2 + HEADS * Q_BLOCK * 2 * Q_BLOCK * 4
                  + 2 * step_rows * GROUP_WIDTH * 4)
    temps = nb * HEADS * Q_BLOCK * (2 * Q_BLOCK * (4 + 2) + V7X_LANES * 4)
    return pl.pallas_call(
        functools.partial(_attn_prompt_kernel, nb=nb),
        out_shape=[jax.ShapeDtypeStruct((d, rows, GROUP_WIDTH), F32)] * 2,
        grid=(d, rows // step_rows),
        in_specs=[spec(0, False), spec(1, True), spec(1, False), spec(2, True), spec(2, False), bias_spec],
        out_specs=[out_spec, out_spec],
        compiler_params=_params(2, blocks + temps),
        name="attn_prompt_g%d" % g,
    )(qkv, qkv, qkv, qkv, qkv, band_bias)


STEP_HEADS = 4


def _decode_attn_item(q_cols, kn_cols, vn_cols, kc_ref, vc_ref, bias_ref, bnew_ref, ko_ref, vo_ref):
    length = kc_ref.shape[-1]
    last_tile = slice(length - V7X_LANES, length)
    is_last = lax.broadcasted_iota(jnp.int32, (1, V7X_LANES), 1) == V7X_LANES - 1

    def rolled(ref, hh, new_col):
        x = pltpu.roll(ref[hh], length - 1, 1)
        return x[:, :length - V7X_LANES], jnp.where(is_last, new_col, x[:, last_tile])

    outs, lses = [], []
    for hh in range(STEP_HEADS):
        col = slice(hh, hh + 1)
        q = q_cols[:, col] * ATTN_SCALE
        k_new, v_new = kn_cols[:, col], vn_cols[:, col]
        s = jnp.sum(q * kc_ref[hh], axis=0, keepdims=True) + bias_ref[col, :]
        s_new = jnp.sum(q * k_new, axis=0, keepdims=True) + bnew_ref[col, 0:1]
        m = jnp.maximum(jnp.max(s, axis=-1, keepdims=True), s_new)
        e = jnp.exp(s - m)
        e_new = jnp.exp(s_new - m)
        den = jnp.sum(e, axis=-1, keepdims=True) + e_new
        acc = jnp.sum(e * vc_ref[hh], axis=-1, keepdims=True) + e_new * v_new
        outs.append(acc * (1.0 / den))
        lses.append(jnp.broadcast_to(m + jnp.log(den), (HEAD_DIM, 1)))
        for cache_ref, out_ref, new_col in ((kc_ref, ko_ref, k_new), (vc_ref, vo_ref, v_new)):
            body, tail = rolled(cache_ref, hh, new_col)
            if length > V7X_LANES:
                out_ref[hh, :, :length - V7X_LANES] = body
            out_ref[hh, :, last_tile] = tail
    return outs, lses


QUADS = HEADS // STEP_HEADS
DECODE_IN_LANES = N_GROUPS * 3 * STEP_HEADS
DECODE_OUT_LANES = N_GROUPS * 2 * STEP_HEADS


def _decode_attn_all(in_refs, out_refs):
    cols = in_refs[0][...]
    quad = lambda first: cols[:, first * STEP_HEADS:(first + 1) * STEP_HEADS]
    lane = lax.broadcasted_iota(jnp.int32, (1, DECODE_OUT_LANES), 1)
    packed = jnp.zeros((HEAD_DIM, DECODE_OUT_LANES), F32)
    for g in range(N_GROUPS):
        kc, vc, bias, bnew = in_refs[1 + 4 * g:5 + 4 * g]
        ko, vo = out_refs[1 + 2 * g:3 + 2 * g]
        outs, lses = _decode_attn_item(quad(3 * g), quad(3 * g + 1), quad(3 * g + 2), kc, vc, bias, bnew, ko, vo)
        for hh in range(STEP_HEADS):
            packed = jnp.where(lane == (2 * g) * STEP_HEADS + hh, outs[hh], packed)
            packed = jnp.where(lane == (2 * g + 1) * STEP_HEADS + hh, lses[hh], packed)
    out_refs[0][...] = packed


class _DecodeJob(NamedTuple):
    cols: jax.Array
    caches: tuple
    biases: tuple
    bias_new: jax.Array
    layer: int


def _decode_job(qkv_rows, cache_k, cache_v, step_bias, new_bias, layer):
    batch = qkv_rows.shape[0]
    cols = qkv_rows.reshape(batch, N_GROUPS * 3, QUADS, STEP_HEADS, HEAD_DIM)
    cols = jnp.transpose(cols, (0, 2, 4, 1, 3)).reshape(batch, QUADS, HEAD_DIM, DECODE_IN_LANES)
    quad = lambda c: c.reshape(c.shape[:2] + (QUADS, STEP_HEADS) + c.shape[3:])
    return _DecodeJob(cols, tuple((quad(k), quad(v)) for k, v in zip(cache_k, cache_v)),
                      tuple(b.reshape(QUADS, STEP_HEADS, -1) for b in step_bias),
                      new_bias.reshape(N_GROUPS, QUADS, STEP_HEADS, V7X_LANES), layer)


def _decode_job_specs(job, item_of_step):
    batch = job.cols.shape[0]
    item = lambda *idx: tuple(item_of_step(*idx))
    operands = [job.cols]
    in_specs = [pl.BlockSpec((None, None, HEAD_DIM, DECODE_IN_LANES), lambda *idx: item(*idx) + (0, 0))]
    out_shapes = [jax.ShapeDtypeStruct((batch, QUADS, HEAD_DIM, DECODE_OUT_LANES), F32)]
    out_specs = [pl.BlockSpec((None, None, HEAD_DIM, DECODE_OUT_LANES), lambda *idx: item(*idx) + (0, 0))]
    vmem = 0
    for g, ((kc, vc), bias) in enumerate(zip(job.caches, job.biases)):
        length = kc.shape[-1]
        block = (None, None, None, STEP_HEADS, HEAD_DIM, length)
        cache_in = pl.BlockSpec(block, lambda *idx: (job.layer,) + item(*idx) + (0, 0, 0))
        cache_out = pl.BlockSpec(block, lambda *idx: (0,) + item(*idx) + (0, 0, 0))
        operands += [kc, vc, bias, job.bias_new]
        in_specs += [cache_in, cache_in,
                     pl.BlockSpec((None, STEP_HEADS, length), lambda *idx: (item(*idx)[1], 0, 0)),
                     pl.BlockSpec((None, None, STEP_HEADS, V7X_LANES),
                                  functools.partial(lambda *idx, g: (g, item(*idx)[1], 0, 0), g=g))]
        out_shapes += [jax.ShapeDtypeStruct((1,) + kc.shape[1:], F32)] * 2
        out_specs += [cache_out, cache_out]
        vmem += 10 * STEP_HEADS * HEAD_DIM * length * 4
    return operands, in_specs, out_shapes, out_specs, vmem


def _decode_job_results(outs):
    cols = outs[0]
    batch = cols.shape[0]
    cols = cols.reshape(batch, QUADS, HEAD_DIM, N_GROUPS, 2, STEP_HEADS)
    rows = jnp.transpose(cols, (3, 4, 0, 1, 5, 2)).reshape(N_GROUPS, 2, 1, batch, GROUP_WIDTH)
    whole = lambda c: c.reshape(c.shape[:2] + (HEADS,) + c.shape[4:])
    return ([(rows[g, 0], rows[g, 1]) for g in range(N_GROUPS)],
            [(whole(outs[1 + 2 * g]), whole(outs[2 + 2 * g])) for g in range(N_GROUPS)])


def _merged_attention(refs, slab_ref, start, n, dilations):
    n_slabs = GROUP_WIDTH // V7X_LANES

    def token_order(ref, d, slot):
        if d == 1:
            return ref[0, start:start + n, :]
        part = slice(start // d, (start + n) // d)
        for r in range(d):
            for c in range(n_slabs):
                slab_ref[slot, c, pl.ds(r, n // d, stride=d), :] = ref[r, part, c * V7X_LANES:(c + 1) * V7X_LANES]
        return jnp.concatenate([slab_ref[slot, c, 0:n, :] for c in range(n_slabs)], axis=1)

    os_ = [token_order(refs[g], d, g) for g, d in enumerate(dilations)]
    ls = [token_order(refs[N_GROUPS + g], d, N_GROUPS + g) for g, d in enumerate(dilations)]
    m = jnp.maximum(jnp.maximum(ls[0], ls[1]), ls[2])
    es = [jnp.exp(l - m) for l in ls]
    inv = 1.0 / (es[0] + es[1] + es[2])
    return es[0] * inv * os_[0] + es[1] * inv * os_[1] + es[2] * inv * os_[2]


def _merge_kernel(x_ref, *refs, tm, dilations):
    w_ref, xo_ref, slab_ref = refs[2 * N_GROUPS:]
    o = _merged_attention(refs[:2 * N_GROUPS], slab_ref, 0, tm, dilations)
    xo_ref[...] = x_ref[...] + jnp.dot(o.astype(BF16), w_ref[...], preferred_element_type=F32)


def _merge(x, outs, lses, w, dilations, name):
    rows = x.shape[0]
    tm = min(ROW_TILE, rows)
    row_spec = pl.BlockSpec((tm, D_MODEL), lambda i: (i, 0))
    grp_specs = [pl.BlockSpec((d, tm // d, GROUP_WIDTH), lambda i: (0, i, 0)) for d in dilations]
    tiles = tm * D_MODEL * 4 * 4 + tm * GROUP_WIDTH * 4 * (2 * 6 + 6) + GROUP_WIDTH * D_MODEL * 2
    temps = tm * GROUP_WIDTH * 4 * 8
    return pl.pallas_call(
        functools.partial(_merge_kernel, tm=tm, dilations=dilations),
        out_shape=jax.ShapeDtypeStruct((rows, D_MODEL), F32),
        grid=(rows // tm,),
        in_specs=[row_spec] + grp_specs * 2 + [_resident((GROUP_WIDTH, D_MODEL))],
        out_specs=row_spec,
        scratch_shapes=[pltpu.VMEM((2 * N_GROUPS, GROUP_WIDTH // V7X_LANES, tm, V7X_LANES), F32)],
        compiler_params=_params(1, tiles + temps, rows),
        name=name,
    )(x, *outs, *lses, w)


def kernel(x_prompt, x_sample, state_pool, cache_k_w128, cache_v_w128, cache_k_w512, cache_v_w512,
           cache_k_w2048, cache_v_w2048, ffn1_norm, ffn1_w_gate, ffn1_w_up, ffn1_w_down, mix_norm,
           pool_w_in, pool_w_group, pool_scale, pool_w_out, attn_w_qkv, attn_w_out, rel_bias,
           ffn2_norm, ffn2_w_gate, ffn2_w_up, ffn2_w_down, final_norm):
    seq = x_prompt.shape[1]
    batch = x_sample.shape[0]
    past_len = cache_k_w2048.shape[2]
    cache_k = (cache_k_w128, cache_k_w512, cache_k_w2048)
    cache_v = (cache_v_w128, cache_v_w512, cache_v_w2048)
    bf = lambda w: w.astype(BF16)
    vec = lambda v: v.reshape(1, D_MODEL)

    xp = x_prompt.reshape(seq, D_MODEL)
    xs = x_sample.reshape(batch, D_MODEL)
    band_bias, *step_bias, new_bias = _bias_tables(rel_bias)
    to_pos_minor = lambda c: jnp.transpose(c, (0, 1, 3, 4, 2))
    from_pos_minor = lambda c: jnp.transpose(c, (0, 1, 4, 2, 3))

    ffn_f32 = {1: (ffn1_norm[:, None], ffn1_w_gate, ffn1_w_up, ffn1_w_down),
               2: (ffn2_norm[:, None], ffn2_w_gate, ffn2_w_up, ffn2_w_down)}

    def ffn_step(x, which, layer, post_g, post, name):
        post_g = None if post_g is None else vec(post_g)
        return _ffn_step(x, *ffn_f32[which], layer, post_g, post, name)

    def ffn(x, which, layer, weights, post_g, post, name, **extra):
        post_g = None if post_g is None else vec(post_g)
        return _ffn(x, ffn_f32[which][0][layer], *weights, post_g, post, name, **extra)

    wout = bf(attn_w_out[0])
    no_dilation = (1,) * N_GROUPS

    (xs, hs), w_ffn = ffn_step(xs, 1, 0, mix_norm[0], "mixer_input", "ffn1_l0_step")
    (xp, hp), _ = ffn(xp, 1, 0, w_ffn, mix_norm[0], "mixer_input", "ffn1_l0_prompt")
    xs, pool_state_s, (w_in, w_grp, w_out) = _pool_step(xs, hs, jnp.swapaxes(state_pool[0], 0, 1), pool_w_in[0],
                                                      pool_w_group[0], vec(pool_scale[0]), pool_w_out[0], past_len)
    pool_state_s = jnp.swapaxes(pool_state_s, 0, 1)[None]
    xp, tail = _pool_prompt(xp, hp, w_in, w_grp, vec(pool_scale[0]), w_out)
    pool_state_p = tail[1:][None, None]

    (xs,), w_ffn2_l0 = ffn_step(xs, 2, 0, None, "none", "ffn2_l0_step")
    (xs, hs), w_ffn1_l1 = ffn_step(xs, 1, 1, mix_norm[1], "attn_input", "ffn1_l1_step")
    qkv_s, wqkv = _qkv_step(hs[0], attn_w_qkv[0])
    decode = _decode_job(qkv_s, [to_pos_minor(c) for c in cache_k], [to_pos_minor(c) for c in cache_v],
                         step_bias, new_bias, 0)

    (xp,), (decoded, rolled) = ffn(xp, 2, 0, w_ffn2_l0, None, "none", "ffn2_l0_prompt", decode=decode)

    (xp, *hp), _ = ffn(xp, 1, 1, w_ffn1_l1, mix_norm[1], "attn_input", "ffn1_l1_prompt",
                       emit_dilations=ATTN_DILATIONS)
    qkv_p = _qkv(hp, wqkv, ATTN_SCALE * LOG2E, BF16, "qkv_prompt")
    outs_p, lses_p, kv_prompt = [], [], []
    for g, w in enumerate(ATTN_WINDOWS):
        o, lse = _attn_prompt(qkv_p[g], band_bias, g)
        outs_p.append(o)
        lses_p.append(lse)
        keep = min(w, seq)
        kv = _kv_tail(hp[ATTN_DILATIONS.index(1)][0], wqkv, g, keep).reshape(2, 1, 1, HEADS, HEAD_DIM, keep)
        kv_prompt.append((from_pos_minor(kv[0]), from_pos_minor(kv[1])))

    xs = _merge(xs, [d[0] for d in decoded], [d[1] for d in decoded], wout, no_dilation, "merge_step")
    (ys,), w_ffn2_l1 = ffn_step(xs, 2, 1, final_norm, "final_norm", "ffn2_l1_step")
    rolled = [from_pos_minor(c) for pair in rolled for c in pair]

    (yp,), _ = ffn(xp, 2, 1, w_ffn2_l1, final_norm, "final_norm", "ffn2_l1_prompt",
                   merge=(outs_p, lses_p, wout, ATTN_DILATIONS))

    return (yp.reshape(x_prompt.shape), ys.reshape(x_sample.shape), pool_state_p, pool_state_s,
            kv_prompt[0][0], kv_prompt[0][1], rolled[0], rolled[1],
            kv_prompt[1][0], kv_prompt[1][1], rolled[2], rolled[3],
            kv_prompt[2][0], kv_prompt[2][1], rolled[4], rolled[5])
```

```python
import functools
import math
from typing import NamedTuple

import numpy as np
import jax
import jax.numpy as jnp
from jax import lax
from jax.experimental import pallas as pl
from jax.experimental.pallas import tpu as pltpu

F32 = jnp.float32
BF16 = jnp.bfloat16

D_MODEL = 1024
D_FF = 2816
RMS_EPS = 1e-6
POOL_WINDOWS = (2, 4, 8, 16)
POOL_GROUP_DIM = D_MODEL // len(POOL_WINDOWS)
POOL_STATE_LEN = max(POOL_WINDOWS) - 1
ATTN_WINDOWS = (128, 512, 2048)
ATTN_DILATIONS = (1, 4, 16)
N_GROUPS = 3
HEAD_DIM = 64
HEADS = 8
GROUP_WIDTH = HEADS * HEAD_DIM
QKV_WIDTH = N_GROUPS * 3 * GROUP_WIDTH
Q_BLOCK = 128
N_KEYS = Q_BLOCK + 1
ATTN_SCALE = HEAD_DIM ** -0.5
LOG2E = math.log2(math.e)
LN2 = math.log(2.0)
N_BUCKETS = 32
MAX_EXACT = N_BUCKETS // 2
MAX_DISTANCE = 2048
NEG_INF = -1e30

V7X_LANES = 128
V7X_VMEM_BYTES = 64 * 1024 * 1024
V7X_VMEM_RESERVE = 6 * 1024 * 1024
ROW_TILE = 1024
POOL_ROW_TILE = 1024
FFN_ROW_TILE = 1024
FFN_MERGE_ROW_TILE = 512
FFN_SUB_ROWS = 256
FFN_MERGE_CHAINS = (256, 256)
DECODE_MAX_ROWS = 128


def _vmem_limit(estimate_bytes):
    return int(min(V7X_VMEM_BYTES - V7X_VMEM_RESERVE, estimate_bytes))


def _params(n_axes, vmem_bytes, rows=None, small_call=False):
    if small_call or (rows is not None and rows < DECODE_MAX_ROWS):
        vmem_bytes = V7X_VMEM_BYTES
    return pltpu.CompilerParams(
        dimension_semantics=("arbitrary",) * n_axes,
        vmem_limit_bytes=_vmem_limit(vmem_bytes))


def _resident(shape):
    return pl.BlockSpec(shape, lambda *_: (0,) * len(shape), pipeline_mode=pl.Buffered(1))


def _rms(x, g):
    return x * lax.rsqrt(jnp.mean(x * x, axis=-1, keepdims=True) + RMS_EPS) * g


class _FfnPlan(NamedTuple):
    pre: str
    post: str
    riders: int
    merge_dilations: tuple
    emit_dilations: tuple
    chains: tuple

    @property
    def n_pre(self):
        return 2 * N_GROUPS + 1 if self.pre == "merge" else 0

    @property
    def n_out(self):
        return {"none": 1, "mixer_input": 2, "final_norm": 1, "attn_input": 1 + len(self.emit_dilations)}[self.post]


def _emit_regrouped(h, out_refs, dilations, slab_ref, start):
    n_slabs = D_MODEL // V7X_LANES
    lanes = lambda c: slice(c * V7X_LANES, (c + 1) * V7X_LANES)
    classes = [(0, h)]
    prev = 1
    for level, (o_ref, d) in enumerate(zip(out_refs, dilations)):
        ratio = d // prev
        if ratio > 1:
            park = slab_ref.at[level % 2]
            finer = []
            for idx, (res, rows) in enumerate(classes):
                n = rows.shape[0]
                for c in range(n_slabs):
                    park[c, idx * n:(idx + 1) * n, :] = rows[:, lanes(c)]
                for q in range(ratio):
                    pick = [park[c, pl.ds(idx * n + q, n // ratio, stride=ratio), :] for c in range(n_slabs)]
                    finer.append((res + prev * q, jnp.concatenate(pick, axis=1)))
            classes, prev = finer, d
        for res, rows in classes:
            n = rows.shape[0]
            o_ref[res, start // d:start // d + n, :] = rows.astype(BF16)


def _ffn_kernel(x_ref, g_ref, wg_ref, wu_ref, wd_ref, *refs, plan):
    n_post = 0 if plan.post == "none" else 1
    n_rider_in, n_rider_out = (1 + 4 * N_GROUPS) * plan.riders, (1 + 2 * N_GROUPS) * plan.riders
    pos = 0
    pre_refs, pos = refs[pos:pos + plan.n_pre], pos + plan.n_pre
    pg_ref, pos = (refs[pos] if n_post else None), pos + n_post
    rider_in, pos = refs[pos:pos + n_rider_in], pos + n_rider_in
    out_refs, pos = refs[pos:pos + plan.n_out], pos + plan.n_out
    rider_out, pos = refs[pos:pos + n_rider_out], pos + n_rider_out
    scratch = refs[pos:]
    tm = x_ref.shape[0]

    if plan.riders:
        _decode_attn_all(rider_in, rider_out)

    assert sum(plan.chains) == tm
    start = 0
    for j, n in enumerate(plan.chains):
        rows = slice(start, start + n)
        x = x_ref[rows, :]
        if plan.pre == "merge":
            attn = _merged_attention(pre_refs[:2 * N_GROUPS], scratch[0].at[j], start, n, plan.merge_dilations)
            x = x + jnp.dot(attn.astype(BF16), pre_refs[-1][...], preferred_element_type=F32)
        h = _rms(x, g_ref[...]).astype(BF16)
        gate = jnp.dot(h, wg_ref[...], preferred_element_type=F32)
        up = jnp.dot(h, wu_ref[...], preferred_element_type=F32)
        a = (gate * jax.nn.sigmoid(gate) * up).astype(BF16)
        y = x + 0.5 * jnp.dot(a, wd_ref[...], preferred_element_type=F32)
        start += n
        if plan.post == "final_norm":
            out_refs[0][rows, :] = _rms(y, pg_ref[...])
            continue
        out_refs[0][rows, :] = y
        if plan.post == "mixer_input":
            out_refs[1][rows, :] = _rms(y, pg_ref[...]).astype(BF16)
        elif plan.post == "attn_input":
            _emit_regrouped(_rms(y, pg_ref[...]), out_refs[1:], plan.emit_dilations, scratch[-1].at[j], rows.start)


FFN_STEP_CHUNK = 1408


def _ffn_step_kernel(x_ref, g_ref, wg_ref, wu_ref, wd_ref, *refs, post, n_chunks):
    n_post = 0 if post == "none" else 1
    n_y = 1 if post in ("none", "final_norm") else 2
    pg_ref = refs[0] if n_post else None
    y_refs = refs[n_post:n_post + n_y]
    wgo_ref, wuo_ref, wdo_ref, h_ref, acc_ref = refs[n_post + n_y:]
    c = pl.program_id(0)

    @pl.when(c == 0)
    def _():
        h_ref[...] = _rms(x_ref[...], g_ref[...]).astype(BF16)
        acc_ref[...] = jnp.zeros(acc_ref.shape, F32)

    wg, wu, wd = wg_ref[...].astype(BF16), wu_ref[...].astype(BF16), wd_ref[...].astype(BF16)
    wgo_ref[...], wuo_ref[...], wdo_ref[...] = wg, wu, wd
    h = h_ref[...]
    gate = jnp.dot(h, wg, preferred_element_type=F32)
    up = jnp.dot(h, wu, preferred_element_type=F32)
    a = (gate * jax.nn.sigmoid(gate) * up).astype(BF16)
    acc_ref[...] += jnp.dot(a, wd, preferred_element_type=F32)

    @pl.when(c == n_chunks - 1)
    def _():
        y = x_ref[...] + 0.5 * acc_ref[...]
        if post == "final_norm":
            y_refs[0][...] = _rms(y, pg_ref[...])
        else:
            y_refs[0][...] = y
            if post == "mixer_input":
                y_refs[1][...] = _rms(y, pg_ref[...]).astype(BF16)
            elif post == "attn_input":
                y_refs[1][0] = _rms(y, pg_ref[...]).astype(BF16)


def _ffn_step(x, g, wg, wu, wd, layer, post_g, post, name):
    rows = x.shape[0]
    ck = FFN_STEP_CHUNK
    n_chunks = D_FF // ck
    whole = lambda shape: pl.BlockSpec(shape, lambda c: (0,) * len(shape))
    out_shape = [jax.ShapeDtypeStruct((rows, D_MODEL), F32)]
    out_specs = [whole((rows, D_MODEL))]
    if post == "mixer_input":
        out_shape.append(jax.ShapeDtypeStruct((rows, D_MODEL), BF16))
        out_specs.append(whole((rows, D_MODEL)))
    elif post == "attn_input":
        out_shape.append(jax.ShapeDtypeStruct((1, rows, D_MODEL), BF16))
        out_specs.append(whole((1, rows, D_MODEL)))
    out_shape += [jax.ShapeDtypeStruct((D_MODEL, D_FF), BF16)] * 2 + [jax.ShapeDtypeStruct((D_FF, D_MODEL), BF16)]
    out_specs += [pl.BlockSpec((D_MODEL, ck), lambda c: (0, c))] * 2 + [pl.BlockSpec((ck, D_MODEL), lambda c: (c, 0))]
    post_in = () if post == "none" else (post_g,)
    outs = pl.pallas_call(
        functools.partial(_ffn_step_kernel, post=post, n_chunks=n_chunks),
        out_shape=out_shape,
        grid=(n_chunks,),
        in_specs=[whole((rows, D_MODEL)), pl.BlockSpec((None, 1, D_MODEL), lambda c: (layer, 0, 0)),
                  pl.BlockSpec((None, D_MODEL, ck), lambda c: (layer, 0, c)),
                  pl.BlockSpec((None, D_MODEL, ck), lambda c: (layer, 0, c)),
                  pl.BlockSpec((None, ck, D_MODEL), lambda c: (layer, c, 0))]
        + [whole((1, D_MODEL))] * len(post_in),
        out_specs=out_specs,
        scratch_shapes=[pltpu.VMEM((rows, D_MODEL), BF16), pltpu.VMEM((rows, D_MODEL), F32)],
        compiler_params=_params(1, 0, rows),
        name=name,
    )(x, g, wg, wu, wd, *post_in)
    return outs[:-3], outs[-3:]


def _ffn(x, g, wg, wu, wd, post_g, post, name, decode=None, merge=None, emit_dilations=()):
    rows = x.shape[0]
    tm = min(FFN_MERGE_ROW_TILE if merge else FFN_ROW_TILE, rows)
    rider_in, rider_in_specs, rider_out_shapes, rider_out_specs, rider_bytes = [], [], [], [], 0
    if decode is not None:
        tm = rows // (decode.cols.shape[0] * QUADS)
        rider_in, rider_in_specs, rider_out_shapes, rider_out_specs, rider_bytes = _decode_job_specs(
            decode, lambda i: (i // QUADS, i % QUADS))
    row_spec = pl.BlockSpec((tm, D_MODEL), lambda i: (i, 0))
    vec_spec = _resident((1, D_MODEL))
    if merge:
        chains = tuple(FFN_MERGE_CHAINS)
    else:
        chains = (min(tm, FFN_SUB_ROWS),) * max(1, tm // FFN_SUB_ROWS)
    sub = max(chains)
    plan = _FfnPlan(pre="merge" if merge else "none", post=post, riders=0 if decode is None else 1,
                    merge_dilations=tuple(merge[3]) if merge else (), emit_dilations=tuple(emit_dilations),
                    chains=chains)
    pre_in, pre_specs, scratch, extra_bytes = [], [], [], 0
    if merge:
        outs_g, lses_g, w_out, dilations = merge
        grp_specs = [pl.BlockSpec((d, tm // d, GROUP_WIDTH), lambda i: (0, i, 0)) for d in dilations]
        pre_in = list(outs_g) + list(lses_g) + [w_out]
        pre_specs = grp_specs * 2 + [_resident((GROUP_WIDTH, D_MODEL))]
        scratch.append(pltpu.VMEM((len(chains), 2 * N_GROUPS, GROUP_WIDTH // V7X_LANES, sub, V7X_LANES), F32))
        extra_bytes += tm * GROUP_WIDTH * 4 * (2 * 6 + 6 + 4)
    out_shape = [jax.ShapeDtypeStruct((rows, D_MODEL), F32)]
    out_specs = [row_spec]
    if post == "mixer_input":
        out_shape.append(jax.ShapeDtypeStruct((rows, D_MODEL), BF16))
        out_specs.append(row_spec)
    elif post == "attn_input":
        out_shape += [jax.ShapeDtypeStruct((d, rows // d, D_MODEL), BF16) for d in emit_dilations]
        out_specs += [pl.BlockSpec((d, tm // d, D_MODEL), lambda i: (0, i, 0)) for d in emit_dilations]
        scratch.append(pltpu.VMEM((len(chains), 2, D_MODEL // V7X_LANES, sub, V7X_LANES), F32))
        extra_bytes += tm * D_MODEL * (4 + 2 * 2 * len(emit_dilations))
    weights = 3 * D_MODEL * D_FF * 2
    tiles = tm * D_MODEL * (4 * 2 + 4 * 2 + 2 * 2)
    temps = 2 * (sub * D_FF * (4 + 4 + 4 + 2) + sub * D_MODEL * 12)
    post_in = () if post == "none" else (post_g,)
    outs = pl.pallas_call(
        functools.partial(_ffn_kernel, plan=plan),
        out_shape=out_shape + rider_out_shapes,
        grid=(rows // tm,),
        in_specs=[row_spec, vec_spec, _resident((D_MODEL, D_FF)), _resident((D_MODEL, D_FF)),
                  _resident((D_FF, D_MODEL))] + pre_specs + [vec_spec] * len(post_in) + rider_in_specs,
        out_specs=out_specs + rider_out_specs,
        scratch_shapes=scratch,
        compiler_params=_params(1, weights + tiles + temps + rider_bytes + extra_bytes, rows),
        name=name,
    )(x, g, wg, wu, wd, *pre_in, *post_in, *rider_in)
    n_own = len(out_shape)
    return outs[:n_own], (_decode_job_results(outs[n_own:]) if decode is not None else None)


def _pool_tail(x, u, sums, counts, wgrp_ref, scale_ref, wout_ref):
    y = x
    for g, w in enumerate(POOL_WINDOWS):
        cols = slice(g * POOL_GROUP_DIM, (g + 1) * POOL_GROUP_DIM)
        z = (sums[g] / counts[g] - u[:, cols]).astype(BF16)
        z = jnp.dot(z, wgrp_ref[g], preferred_element_type=F32) * scale_ref[:, cols]
        y = y + jnp.dot(z.astype(BF16), wout_ref[cols, :], preferred_element_type=F32)
    return y


def _pool_prompt_kernel(x_ref, h_ref, win_ref, wgrp_ref, scale_ref, wout_ref,
                        xo_ref, tail_ref, hist_ref, *, tm):
    i = pl.program_id(0)
    hist = POOL_STATE_LEN + 1

    @pl.when(i == 0)
    def _():
        hist_ref[...] = jnp.zeros((hist, D_MODEL), F32)

    u = jnp.dot(h_ref[...], win_ref[...], preferred_element_type=F32)
    pos = i * tm + lax.broadcasted_iota(jnp.int32, (tm, 1), 0)
    sums, counts = [], []
    for g, w in enumerate(POOL_WINDOWS):
        cols = slice(g * POOL_GROUP_DIM, (g + 1) * POOL_GROUP_DIM)
        s = jnp.concatenate([hist_ref[:, cols], u[:, cols]], axis=0)
        k = 1
        while k < w:
            s = s + pltpu.roll(s, k, 0)
            k *= 2
        sums.append(s[hist:, :])
        counts.append(jnp.minimum(w, pos + 1).astype(F32))
    xo_ref[...] = _pool_tail(x_ref[...], u, sums, counts, wgrp_ref, scale_ref, wout_ref)
    last = u[tm - hist:, :]
    tail_ref[...] = last
    hist_ref[...] = last


def _pool_prompt(x, h, win, wgrp, scale, wout):
    rows = x.shape[0]
    tm = POOL_ROW_TILE
    hist = POOL_STATE_LEN + 1
    row_spec = pl.BlockSpec((tm, D_MODEL), lambda i: (i, 0))
    weights = (2 * D_MODEL * D_MODEL + 4 * POOL_GROUP_DIM * POOL_GROUP_DIM) * 2
    tiles = tm * D_MODEL * (4 * 2 + 2 * 2 + 4 * 2) + (tm + hist) * D_MODEL * 4
    temps = tm * D_MODEL * 4 * 6
    return pl.pallas_call(
        functools.partial(_pool_prompt_kernel, tm=tm),
        out_shape=[jax.ShapeDtypeStruct((rows, D_MODEL), F32),
                   jax.ShapeDtypeStruct((hist, D_MODEL), F32)],
        grid=(rows // tm,),
        in_specs=[row_spec, row_spec, _resident((D_MODEL, D_MODEL)),
                  _resident((4, POOL_GROUP_DIM, POOL_GROUP_DIM)), _resident((1, D_MODEL)),
                  _resident((D_MODEL, D_MODEL))],
        out_specs=[row_spec, pl.BlockSpec((hist, D_MODEL), lambda i: (0, 0))],
        scratch_shapes=[pltpu.VMEM((hist, D_MODEL), F32)],
        compiler_params=_params(1, weights + tiles + temps),
        name="pool_prompt",
    )(x, h, win, wgrp, scale, wout)


def _pool_step_kernel(x_ref, h_ref, st_ref, win32_ref, wgrp32_ref, scale_ref, wout32_ref,
                      xo_ref, sto_ref, win_ref, wgrp_ref, wout_ref, *, past_len):
    win_ref[...] = win32_ref[...].astype(BF16)
    wgrp_ref[...] = wgrp32_ref[...].astype(BF16)
    wout_ref[...] = wout32_ref[...].astype(BF16)
    u = jnp.dot(h_ref[...], win_ref[...], preferred_element_type=F32)
    sums, counts = [], []
    for g, w in enumerate(POOL_WINDOWS):
        cols = slice(g * POOL_GROUP_DIM, (g + 1) * POOL_GROUP_DIM)
        acc = u[:, cols]
        for j in range(1, w):
            acc = acc + st_ref[POOL_STATE_LEN - j, :, cols]
        sums.append(acc)
        counts.append(float(min(w, past_len + 1)))
    xo_ref[...] = _pool_tail(x_ref[...], u, sums, counts, wgrp_ref, scale_ref, wout_ref)
    for k in range(POOL_STATE_LEN - 1):
        sto_ref[k] = st_ref[k + 1]
    sto_ref[POOL_STATE_LEN - 1] = u


def _pool_step(x, h, state, win, wgrp, scale, wout, past_len):
    rows = x.shape[0]
    full = lambda shape: pl.BlockSpec(shape, lambda i: (0,) * len(shape))
    weight_shapes = [(D_MODEL, D_MODEL), (4, POOL_GROUP_DIM, POOL_GROUP_DIM), (D_MODEL, D_MODEL)]
    outs = pl.pallas_call(
        functools.partial(_pool_step_kernel, past_len=past_len),
        out_shape=[jax.ShapeDtypeStruct((rows, D_MODEL), F32), jax.ShapeDtypeStruct(state.shape, F32)]
        + [jax.ShapeDtypeStruct(s, BF16) for s in weight_shapes],
        grid=(1,),
        in_specs=[full((rows, D_MODEL)), full((rows, D_MODEL)), full(state.shape),
                  full(weight_shapes[0]), full(weight_shapes[1]), full((1, D_MODEL)), full(weight_shapes[2])],
        out_specs=[full((rows, D_MODEL)), full(state.shape)] + [full(s) for s in weight_shapes],
        compiler_params=_params(1, 0, rows),
        name="pool_step",
    )(x, h, state, win, wgrp, scale, wout)
    return outs[0], outs[1], outs[2:]


def _qkv_kernel(*refs, q_scale):
    hs, w_ref, outs = refs[:N_GROUPS], refs[N_GROUPS], refs[N_GROUPS + 1:]
    width = 3 * GROUP_WIDTH
    for g, (h_ref, o_ref) in enumerate(zip(hs, outs)):
        d, sub, _ = h_ref.shape
        h = h_ref[...].reshape(d * sub, D_MODEL)
        res = jnp.dot(h, w_ref[:, g * width:(g + 1) * width], preferred_element_type=F32)
        for r in range(d):
            rows = slice(r * sub, (r + 1) * sub)
            o_ref[r, :, 0:GROUP_WIDTH] = (res[rows, 0:GROUP_WIDTH] * q_scale).astype(o_ref.dtype)
            o_ref[r, :, GROUP_WIDTH:] = res[rows, GROUP_WIDTH:].astype(o_ref.dtype)


def _qkv(hs, w, q_scale, dtype, name):
    rows = hs[0].shape[0] * hs[0].shape[1]
    tm = min(ROW_TILE, rows)
    width = 3 * GROUP_WIDTH
    weights = D_MODEL * QKV_WIDTH * 2
    tiles = N_GROUPS * tm * D_MODEL * 2 * 2 + tm * QKV_WIDTH * jnp.dtype(dtype).itemsize * 2
    temps = tm * width * 4 * 2
    return pl.pallas_call(
        functools.partial(_qkv_kernel, q_scale=q_scale),
        out_shape=[jax.ShapeDtypeStruct(h.shape[:2] + (width,), dtype) for h in hs],
        grid=(rows // tm,),
        in_specs=[pl.BlockSpec((h.shape[0], tm // h.shape[0], D_MODEL), lambda i: (0, i, 0)) for h in hs]
        + [_resident((D_MODEL, QKV_WIDTH))],
        out_specs=[pl.BlockSpec((h.shape[0], tm // h.shape[0], width), lambda i: (0, i, 0)) for h in hs],
        compiler_params=_params(1, weights + tiles + temps, rows),
        name=name,
    )(*hs, w)


def _qkv_step_kernel(h_ref, w32_ref, o_ref, w_ref):
    w = w32_ref[...].astype(BF16)
    w_ref[...] = w
    o_ref[...] = jnp.dot(h_ref[...], w, preferred_element_type=F32)


def _qkv_step(h, w):
    rows = h.shape[0]
    ck = 3 * GROUP_WIDTH
    return pl.pallas_call(
        _qkv_step_kernel,
        out_shape=[jax.ShapeDtypeStruct((rows, QKV_WIDTH), F32), jax.ShapeDtypeStruct((D_MODEL, QKV_WIDTH), BF16)],
        grid=(QKV_WIDTH // ck,),
        in_specs=[pl.BlockSpec((rows, D_MODEL), lambda c: (0, 0)), pl.BlockSpec((D_MODEL, ck), lambda c: (0, c))],
        out_specs=[pl.BlockSpec((rows, ck), lambda c: (0, c)), pl.BlockSpec((D_MODEL, ck), lambda c: (0, c))],
        compiler_params=_params(1, 0, rows),
        name="qkv_step",
    )(h, w)


def _kv_tail_kernel(h_ref, wk_ref, wv_ref, o_ref):
    h = h_ref[...]
    o_ref[0] = jnp.dot(h, wk_ref[...], preferred_element_type=F32).T
    o_ref[1] = jnp.dot(h, wv_ref[...], preferred_element_type=F32).T


def _kv_tail(h, w, g, window):
    rows = h.shape[0]
    tm = min(window, 256)
    first = (rows - window) // tm
    col = 3 * g
    blocks = 2 * (tm * D_MODEL * 2 + 2 * D_MODEL * GROUP_WIDTH * 2 + 2 * GROUP_WIDTH * tm * 4)
    temps = 4 * tm * GROUP_WIDTH * 4
    return pl.pallas_call(
        _kv_tail_kernel,
        out_shape=jax.ShapeDtypeStruct((2, GROUP_WIDTH, window), F32),
        grid=(window // tm,),
        in_specs=[pl.BlockSpec((tm, D_MODEL), lambda i: (first + i, 0)),
                  pl.BlockSpec((D_MODEL, GROUP_WIDTH), lambda i: (0, col + 1)),
                  pl.BlockSpec((D_MODEL, GROUP_WIDTH), lambda i: (0, col + 2))],
        out_specs=pl.BlockSpec((2, GROUP_WIDTH, tm), lambda i: (0, 0, i)),
        compiler_params=_params(1, blocks + temps, small_call=True),
        name="kv_tail_g%d" % g,
    )(h, w, w)


def _bucket_table():
    out = np.zeros((N_GROUPS, N_KEYS), np.int32)
    for g, d in enumerate(ATTN_DILATIONS):
        dist = np.arange(N_KEYS, dtype=np.int32) * d
        distf = np.maximum(dist, 1).astype(np.float32)
        log_b = MAX_EXACT + (np.log(distf / np.float32(MAX_EXACT)) / np.float32(math.log(MAX_DISTANCE / MAX_EXACT))
                             * np.float32(N_BUCKETS - MAX_EXACT)).astype(np.int32)
        log_b = np.minimum(log_b, N_BUCKETS - 1)
        out[g] = np.where(dist < MAX_EXACT, dist, log_b)
    return out


def _band_offsets():
    a = np.arange(Q_BLOCK)[:, None]
    b = np.arange(2 * Q_BLOCK)[None, :]
    j = Q_BLOCK + a - b
    return np.where((j >= 0) & (j <= Q_BLOCK), j, -1).astype(np.int32)


def _bias_kernel(tab_ref, bidx_ref, l0_ref, l1_ref, l2_ref, band_ref, s0_ref, s1_ref, s2_ref, new_ref, *, buckets):
    sub = lax.broadcasted_iota(jnp.int32, (HEADS, V7X_LANES), 0)
    for g, (lidx_ref, step_ref) in enumerate(((l0_ref, s0_ref), (l1_ref, s1_ref), (l2_ref, s2_ref))):
        bidx = bidx_ref[g]
        lidx = lidx_ref[...]
        used = sorted(set(int(v) for v in buckets[g]))
        t = jnp.zeros((HEADS, V7X_LANES), F32)
        for h in range(HEADS):
            t = jnp.where(sub == h, tab_ref[int(buckets[g][0]), g * HEADS + h], t)
        new_ref[g] = t
        for h in range(HEADS):
            tile = jnp.full(bidx.shape, NEG_INF, F32)
            row = jnp.full(lidx.shape, NEG_INF, F32)
            for v in used:
                tile = jnp.where(bidx == v, tab_ref[v, g * HEADS + h], tile)
                row = jnp.where(lidx == v, tab_ref[v, g * HEADS + h], row)
            band_ref[g, h] = tile * LOG2E
            step_ref[h:h + 1, :] = row


def _bias_tables(rel_bias):
    buckets = _bucket_table()
    band = _band_offsets()
    bidx = np.stack([np.where(band >= 0, buckets[g][np.maximum(band, 0)], -1) for g in range(N_GROUPS)])
    lidx = []
    for g, (w, d) in enumerate(zip(ATTN_WINDOWS, ATTN_DILATIONS)):
        pos = np.arange(w)
        lidx.append(np.where(pos % d == 0, buckets[g][(w - pos) // d], -1).astype(np.int32)[None])
    vmem = pl.BlockSpec(memory_space=pltpu.VMEM)
    tile_bytes = N_GROUPS * HEADS * Q_BLOCK * 2 * Q_BLOCK * 4
    return pl.pallas_call(
        functools.partial(_bias_kernel, buckets=buckets),
        out_shape=[jax.ShapeDtypeStruct((N_GROUPS, HEADS, Q_BLOCK, 2 * Q_BLOCK), F32)]
        + [jax.ShapeDtypeStruct((HEADS, w), F32) for w in ATTN_WINDOWS]
        + [jax.ShapeDtypeStruct((N_GROUPS, HEADS, V7X_LANES), F32)],
        in_specs=[pl.BlockSpec(memory_space=pltpu.SMEM)] + [vmem] * 4,
        out_specs=[vmem] * 5,
        compiler_params=_params(0, 4 * tile_bytes, small_call=True),
        name="bias_tables",
    )(rel_bias, jnp.asarray(bidx), *[jnp.asarray(l) for l in lidx])


ATTN_BLOCKS_PER_STEP = 8


def _attn_prompt_kernel(q_ref, kp_ref, kc_ref, vp_ref, vc_ref, bias_ref, o_ref, lse_ref, *, nb):
    low = lax.broadcasted_iota(jnp.int32, (1, V7X_LANES), 1) < HEAD_DIM
    col = lax.broadcasted_iota(jnp.int32, (1, 2 * Q_BLOCK), 1)
    no_prev = jnp.where(jnp.logical_and(pl.program_id(1) == 0, col < Q_BLOCK), NEG_INF, 0.0)
    q = q_ref[...]
    k = jnp.concatenate([kp_ref[...], kc_ref[...]], axis=0)
    v = jnp.concatenate([vp_ref[...], vc_ref[...]], axis=0)
    items = [(b, p, half) for b in range(nb) for p in range(HEADS // 2) for half in range(2)]
    q_rows = lambda b: slice(b * Q_BLOCK, (b + 1) * Q_BLOCK)
    k_rows = lambda b: slice(b * Q_BLOCK, (b + 2) * Q_BLOCK)
    lanes = lambda p: slice(p * V7X_LANES, (p + 1) * V7X_LANES)
    mine = lambda half: low if half == 0 else jnp.logical_not(low)

    scores = {}
    for b, p, half in items:
        qh = jnp.where(mine(half), q[q_rows(b), lanes(p)], jnp.zeros((), BF16))
        s = lax.dot_general(qh, k[k_rows(b), lanes(p)], (((1,), (1,)), ((), ())), preferred_element_type=F32)
        s = s + bias_ref[2 * p + half]
        scores[b, p, half] = s + no_prev if b == 0 else s
    tops = {it: jnp.max(scores[it], axis=-1, keepdims=True) for it in items}
    weights = {it: jnp.exp2(scores[it] - tops[it]).astype(BF16) for it in items}
    v_ones = {(p, half): jnp.where(mine(half), v[:, lanes(p)], jnp.ones((), BF16))
              for p in range(HEADS // 2) for half in range(2)}
    acc = {}
    for b, p, half in items:
        acc[b, p, half] = jnp.dot(weights[b, p, half], v_ones[p, half][k_rows(b), :],
                                  preferred_element_type=F32)
    for b in range(nb):
        for p in range(HEADS // 2):
            a0, a1 = acc[b, p, 0], acc[b, p, 1]
            den = pltpu.roll(jnp.where(low, a1, a0), HEAD_DIM, 1)
            o_ref[q_rows(b), lanes(p)] = jnp.where(low, a0, a1) * (1.0 / den)
            lse_ref[q_rows(b), lanes(p)] = jnp.where(low, tops[b, p, 0], tops[b, p, 1]) * LN2 + jnp.log(den)


def _attn_prompt(qkv, band_bias, g):
    d, rows, _ = qkv.shape
    nb = ATTN_BLOCKS_PER_STEP
    step_rows = nb * Q_BLOCK

    def spec(which, prev):
        if prev:
            return pl.BlockSpec((None, Q_BLOCK, GROUP_WIDTH), lambda r, n: (r, jnp.maximum(nb * n - 1, 0), which))
        return pl.BlockSpec((None, step_rows, GROUP_WIDTH), lambda r, n: (r, n, which))

    out_spec = pl.BlockSpec((None, step_rows, GROUP_WIDTH), lambda r, n: (r, n, 0))
    bias_spec = pl.BlockSpec((None, HEADS, Q_BLOCK, 2 * Q_BLOCK), lambda r, n: (g, 0, 0, 0))
    blocks = 2 * ((3 * step_rows + 2 * Q_BLOCK) * GROUP_WIDTH * 2 + HEADS * Q_BLOCK * 2 * Q_BLOCK * 4
                  + 2 * step_rows * GROUP_WIDTH * 4)
    temps = nb * HEADS * Q_BLOCK * (2 * Q_BLOCK * (4 + 2) + V7X_LANES * 4)
    return pl.pallas_call(
        functools.partial(_attn_prompt_kernel, nb=nb),
        out_shape=[jax.ShapeDtypeStruct((d, rows, GROUP_WIDTH), F32)] * 2,
        grid=(d, rows // step_rows),
        in_specs=[spec(0, False), spec(1, True), spec(1, False), spec(2, True), spec(2, False), bias_spec],
        out_specs=[out_spec, out_spec],
        compiler_params=_params(2, blocks + temps),
        name="attn_prompt_g%d" % g,
    )(qkv, qkv, qkv, qkv, qkv, band_bias)


STEP_HEADS = 4


def _decode_attn_item(q_cols, kn_cols, vn_cols, kc_ref, vc_ref, bias_ref, bnew_ref, ko_ref, vo_ref):
    length = kc_ref.shape[-1]
    last_tile = slice(length - V7X_LANES, length)
    is_last = lax.broadcasted_iota(jnp.int32, (1, V7X_LANES), 1) == V7X_LANES - 1

    def rolled(ref, hh, new_col):
        x = pltpu.roll(ref[hh], length - 1, 1)
        return x[:, :length - V7X_LANES], jnp.where(is_last, new_col, x[:, last_tile])

    outs, lses = [], []
    for hh in range(STEP_HEADS):
        col = slice(hh, hh + 1)
        q = q_cols[:, col] * ATTN_SCALE
        k_new, v_new = kn_cols[:, col], vn_cols[:, col]
        s = jnp.sum(q * kc_ref[hh], axis=0, keepdims=True) + bias_ref[col, :]
        s_new = jnp.sum(q * k_new, axis=0, keepdims=True) + bnew_ref[col, 0:1]
        m = jnp.maximum(jnp.max(s, axis=-1, keepdims=True), s_new)
        e = jnp.exp(s - m)
        e_new = jnp.exp(s_new - m)
        den = jnp.sum(e, axis=-1, keepdims=True) + e_new
        acc = jnp.sum(e * vc_ref[hh], axis=-1, keepdims=True) + e_new * v_new
        outs.append(acc * (1.0 / den))
        lses.append(jnp.broadcast_to(m + jnp.log(den), (HEAD_DIM, 1)))
        for cache_ref, out_ref, new_col in ((kc_ref, ko_ref, k_new), (vc_ref, vo_ref, v_new)):
            body, tail = rolled(cache_ref, hh, new_col)
            if length > V7X_LANES:
                out_ref[hh, :, :length - V7X_LANES] = body
            out_ref[hh, :, last_tile] = tail
    return outs, lses


QUADS = HEADS // STEP_HEADS
DECODE_IN_LANES = N_GROUPS * 3 * STEP_HEADS
DECODE_OUT_LANES = N_GROUPS * 2 * STEP_HEADS


def _decode_attn_all(in_refs, out_refs):
    cols = in_refs[0][...]
    quad = lambda first: cols[:, first * STEP_HEADS:(first + 1) * STEP_HEADS]
    lane = lax.broadcasted_iota(jnp.int32, (1, DECODE_OUT_LANES), 1)
    packed = jnp.zeros((HEAD_DIM, DECODE_OUT_LANES), F32)
    for g in range(N_GROUPS):
        kc, vc, bias, bnew = in_refs[1 + 4 * g:5 + 4 * g]
        ko, vo = out_refs[1 + 2 * g:3 + 2 * g]
        outs, lses = _decode_attn_item(quad(3 * g), quad(3 * g + 1), quad(3 * g + 2), kc, vc, bias, bnew, ko, vo)
        for hh in range(STEP_HEADS):
            packed = jnp.where(lane == (2 * g) * STEP_HEADS + hh, outs[hh], packed)
            packed = jnp.where(lane == (2 * g + 1) * STEP_HEADS + hh, lses[hh], packed)
    out_refs[0][...] = packed


class _DecodeJob(NamedTuple):
    cols: jax.Array
    caches: tuple
    biases: tuple
    bias_new: jax.Array
    layer: int


def _decode_job(qkv_rows, cache_k, cache_v, step_bias, new_bias, layer):
    batch = qkv_rows.shape[0]
    cols = qkv_rows.reshape(batch, N_GROUPS * 3, QUADS, STEP_HEADS, HEAD_DIM)
    cols = jnp.transpose(cols, (0, 2, 4, 1, 3)).reshape(batch, QUADS, HEAD_DIM, DECODE_IN_LANES)
    quad = lambda c: c.reshape(c.shape[:2] + (QUADS, STEP_HEADS) + c.shape[3:])
    return _DecodeJob(cols, tuple((quad(k), quad(v)) for k, v in zip(cache_k, cache_v)),
                      tuple(b.reshape(QUADS, STEP_HEADS, -1) for b in step_bias),
                      new_bias.reshape(N_GROUPS, QUADS, STEP_HEADS, V7X_LANES), layer)


def _decode_job_specs(job, item_of_step):
    batch = job.cols.shape[0]
    item = lambda *idx: tuple(item_of_step(*idx))
    operands = [job.cols]
    in_specs = [pl.BlockSpec((None, None, HEAD_DIM, DECODE_IN_LANES), lambda *idx: item(*idx) + (0, 0))]
    out_shapes = [jax.ShapeDtypeStruct((batch, QUADS, HEAD_DIM, DECODE_OUT_LANES), F32)]
    out_specs = [pl.BlockSpec((None, None, HEAD_DIM, DECODE_OUT_LANES), lambda *idx: item(*idx) + (0, 0))]
    vmem = 0
    for g, ((kc, vc), bias) in enumerate(zip(job.caches, job.biases)):
        length = kc.shape[-1]
        block = (None, None, None, STEP_HEADS, HEAD_DIM, length)
        cache_in = pl.BlockSpec(block, lambda *idx: (job.layer,) + item(*idx) + (0, 0, 0))
        cache_out = pl.BlockSpec(block, lambda *idx: (0,) + item(*idx) + (0, 0, 0))
        operands += [kc, vc, bias, job.bias_new]
        in_specs += [cache_in, cache_in,
                     pl.BlockSpec((None, STEP_HEADS, length), lambda *idx: (item(*idx)[1], 0, 0)),
                     pl.BlockSpec((None, None, STEP_HEADS, V7X_LANES),
                                  functools.partial(lambda *idx, g: (g, item(*idx)[1], 0, 0), g=g))]
        out_shapes += [jax.ShapeDtypeStruct((1,) + kc.shape[1:], F32)] * 2
        out_specs += [cache_out, cache_out]
        vmem += 10 * STEP_HEADS * HEAD_DIM * length * 4
    return operands, in_specs, out_shapes, out_specs, vmem


def _decode_job_results(outs):
    cols = outs[0]
    batch = cols.shape[0]
    cols = cols.reshape(batch, QUADS, HEAD_DIM, N_GROUPS, 2, STEP_HEADS)
    rows = jnp.transpose(cols, (3, 4, 0, 1, 5, 2)).reshape(N_GROUPS, 2, 1, batch, GROUP_WIDTH)
    whole = lambda c: c.reshape(c.shape[:2] + (HEADS,) + c.shape[4:])
    return ([(rows[g, 0], rows[g, 1]) for g in range(N_GROUPS)],
            [(whole(outs[1 + 2 * g]), whole(outs[2 + 2 * g])) for g in range(N_GROUPS)])


def _merged_attention(refs, slab_ref, start, n, dilations):
    n_slabs = GROUP_WIDTH // V7X_LANES

    def token_order(ref, d, slot):
        if d == 1:
            return ref[0, start:start + n, :]
        part = slice(start // d, (start + n) // d)
        for r in range(d):
            for c in range(n_slabs):
                slab_ref[slot, c, pl.ds(r, n // d, stride=d), :] = ref[r, part, c * V7X_LANES:(c + 1) * V7X_LANES]
        return jnp.concatenate([slab_ref[slot, c, 0:n, :] for c in range(n_slabs)], axis=1)

    os_ = [token_order(refs[g], d, g) for g, d in enumerate(dilations)]
    ls = [token_order(refs[N_GROUPS + g], d, N_GROUPS + g) for g, d in enumerate(dilations)]
    m = jnp.maximum(jnp.maximum(ls[0], ls[1]), ls[2])
    es = [jnp.exp(l - m) for l in ls]
    inv = 1.0 / (es[0] + es[1] + es[2])
    return es[0] * inv * os_[0] + es[1] * inv * os_[1] + es[2] * inv * os_[2]


def _merge_kernel(x_ref, *refs, tm, dilations):
    w_ref, xo_ref, slab_ref = refs[2 * N_GROUPS:]
    o = _merged_attention(refs[:2 * N_GROUPS], slab_ref, 0, tm, dilations)
    xo_ref[...] = x_ref[...] + jnp.dot(o.astype(BF16), w_ref[...], preferred_element_type=F32)


def _merge(x, outs, lses, w, dilations, name):
    rows = x.shape[0]
    tm = min(ROW_TILE, rows)
    row_spec = pl.BlockSpec((tm, D_MODEL), lambda i: (i, 0))
    grp_specs = [pl.BlockSpec((d, tm // d, GROUP_WIDTH), lambda i: (0, i, 0)) for d in dilations]
    tiles = tm * D_MODEL * 4 * 4 + tm * GROUP_WIDTH * 4 * (2 * 6 + 6) + GROUP_WIDTH * D_MODEL * 2
    temps = tm * GROUP_WIDTH * 4 * 8
    return pl.pallas_call(
        functools.partial(_merge_kernel, tm=tm, dilations=dilations),
        out_shape=jax.ShapeDtypeStruct((rows, D_MODEL), F32),
        grid=(rows // tm,),
        in_specs=[row_spec] + grp_specs * 2 + [_resident((GROUP_WIDTH, D_MODEL))],
        out_specs=row_spec,
        scratch_shapes=[pltpu.VMEM((2 * N_GROUPS, GROUP_WIDTH // V7X_LANES, tm, V7X_LANES), F32)],
        compiler_params=_params(1, tiles + temps, rows),
        name=name,
    )(x, *outs, *lses, w)


def kernel(x_prompt, x_sample, state_pool, cache_k_w128, cache_v_w128, cache_k_w512, cache_v_w512,
           cache_k_w2048, cache_v_w2048, ffn1_norm, ffn1_w_gate, ffn1_w_up, ffn1_w_down, mix_norm,
           pool_w_in, pool_w_group, pool_scale, pool_w_out, attn_w_qkv, attn_w_out, rel_bias,
           ffn2_norm, ffn2_w_gate, ffn2_w_up, ffn2_w_down, final_norm):
    seq = x_prompt.shape[1]
    batch = x_sample.shape[0]
    past_len = cache_k_w2048.shape[2]
    cache_k = (cache_k_w128, cache_k_w512, cache_k_w2048)
    cache_v = (cache_v_w128, cache_v_w512, cache_v_w2048)
    bf = lambda w: w.astype(BF16)
    vec = lambda v: v.reshape(1, D_MODEL)

    xp = x_prompt.reshape(seq, D_MODEL)
    xs = x_sample.reshape(batch, D_MODEL)
    band_bias, *step_bias, new_bias = _bias_tables(rel_bias)
    to_pos_minor = lambda c: jnp.transpose(c, (0, 1, 3, 4, 2))
    from_pos_minor = lambda c: jnp.transpose(c, (0, 1, 4, 2, 3))

    ffn_f32 = {1: (ffn1_norm[:, None], ffn1_w_gate, ffn1_w_up, ffn1_w_down),
               2: (ffn2_norm[:, None], ffn2_w_gate, ffn2_w_up, ffn2_w_down)}

    def ffn_step(x, which, layer, post_g, post, name):
        post_g = None if post_g is None else vec(post_g)
        return _ffn_step(x, *ffn_f32[which], layer, post_g, post, name)

    def ffn(x, which, layer, weights, post_g, post, name, **extra):
        post_g = None if post_g is None else vec(post_g)
        return _ffn(x, ffn_f32[which][0][layer], *weights, post_g, post, name, **extra)

    wout = bf(attn_w_out[0])
    no_dilation = (1,) * N_GROUPS

    (xs, hs), w_ffn = ffn_step(xs, 1, 0, mix_norm[0], "mixer_input", "ffn1_l0_step")
    (xp, hp), _ = ffn(xp, 1, 0, w_ffn, mix_norm[0], "mixer_input", "ffn1_l0_prompt")
    xs, pool_state_s, (w_in, w_grp, w_out) = _pool_step(xs, hs, jnp.swapaxes(state_pool[0], 0, 1), pool_w_in[0],
                                                      pool_w_group[0], vec(pool_scale[0]), pool_w_out[0], past_len)
    pool_state_s = jnp.swapaxes(pool_state_s, 0, 1)[None]
    xp, tail = _pool_prompt(xp, hp, w_in, w_grp, vec(pool_scale[0]), w_out)
    pool_state_p = tail[1:][None, None]

    (xs,), w_ffn2_l0 = ffn_step(xs, 2, 0, None, "none", "ffn2_l0_step")
    (xs, hs), w_ffn1_l1 = ffn_step(xs, 1, 1, mix_norm[1], "attn_input", "ffn1_l1_step")
    qkv_s, wqkv = _qkv_step(hs[0], attn_w_qkv[0])
    decode = _decode_job(qkv_s, [to_pos_minor(c) for c in cache_k], [to_pos_minor(c) for c in cache_v],
                         step_bias, new_bias, 0)

    (xp,), (decoded, rolled) = ffn(xp, 2, 0, w_ffn2_l0, None, "none", "ffn2_l0_prompt", decode=decode)

    (xp, *hp), _ = ffn(xp, 1, 1, w_ffn1_l1, mix_norm[1], "attn_input", "ffn1_l1_prompt",
                       emit_dilations=ATTN_DILATIONS)
    qkv_p = _qkv(hp, wqkv, ATTN_SCALE * LOG2E, BF16, "qkv_prompt")
    outs_p, lses_p, kv_prompt = [], [], []
    for g, w in enumerate(ATTN_WINDOWS):
        o, lse = _attn_prompt(qkv_p[g], band_bias, g)
        outs_p.append(o)
        lses_p.append(lse)
        keep = min(w, seq)
        kv = _kv_tail(hp[ATTN_DILATIONS.index(1)][0], wqkv, g, keep).reshape(2, 1, 1, HEADS, HEAD_DIM, keep)
        kv_prompt.append((from_pos_minor(kv[0]), from_pos_minor(kv[1])))

    xs = _merge(xs, [d[0] for d in decoded], [d[1] for d in decoded], wout, no_dilation, "merge_step")
    (ys,), w_ffn2_l1 = ffn_step(xs, 2, 1, final_norm, "final_norm", "ffn2_l1_step")
    rolled = [from_pos_minor(c) for pair in rolled for c in pair]

    (yp,), _ = ffn(xp, 2, 1, w_ffn2_l1, final_norm, "final_norm", "ffn2_l1_prompt",
                   merge=(outs_p, lses_p, wout, ATTN_DILATIONS))

    return (yp.reshape(x_prompt.shape), ys.reshape(x_sample.shape), pool_state_p, pool_state_s,
            kv_prompt[0][0], kv_prompt[0][1], rolled[0], rolled[1],
            kv_prompt[1][0], kv_prompt[1][1], rolled[2], rolled[3],
            kv_prompt[2][0], kv_prompt[2][1], rolled[4], rolled[5])
```
